```python
import jax, jax.numpy as jnp
from jax import lax
import numpy as np

D_MODEL = 1024
BATCH = 8
SEQ = 4096
DEPTH = 2
DEC_BATCH = 2
DEC_SEQ = 16384
PAST_LEN = 128

GRID_W = 64
HEAD_DIM = 64
A_HEADS = 8
NA_KH = 8
NA_KW = 16
B_HEADS = 8
B_KV_HEADS = 2
B_GROUP = B_HEADS // B_KV_HEADS
C_HEADS = 16
C_NOPE = 64
C_ROPE = 32
C_V = 64
C_Q_RANK = 384
C_KV_RANK = 256
ROPE_THETA = 10000.0
Q_BLOCK = 128
EPS = 1e-6
N_EVEN = (DEPTH + 1) // 2
N_ODD = DEPTH // 2

A_WIDTH = A_HEADS * HEAD_DIM
B_WIDTH = B_HEADS * HEAD_DIM
B_KV_WIDTH = B_KV_HEADS * HEAD_DIM
MIX0_WIDTH = A_WIDTH + B_WIDTH
IN0_WIDTH = 3 * A_WIDTH + B_WIDTH + 2 * B_KV_WIDTH + MIX0_WIDTH
C_QK_DIM = C_NOPE + C_ROPE
C_WIDTH = C_HEADS * C_V
IN1_WIDTH = C_Q_RANK + C_KV_RANK + C_ROPE + C_WIDTH

kernel_name = "hybrid_natten_gqa_mla_gated_encoder"


def rms_norm(x, g):
    xf = x.astype(jnp.float32)
    y = xf * lax.rsqrt(jnp.mean(xf * xf, axis=-1, keepdims=True) + EPS)
    return (y * g.astype(jnp.float32)).astype(x.dtype)


def axial_angles(n_tok, rot_dim):
    n_freq = rot_dim // 4
    inv = ROPE_THETA ** (-jnp.arange(n_freq, dtype=jnp.float32) / n_freq)
    t = jnp.arange(n_tok, dtype=jnp.int32)
    row = (t // GRID_W).astype(jnp.float32)
    col = (t % GRID_W).astype(jnp.float32)
    ang = jnp.concatenate([row[:, None] * inv[None], col[:, None] * inv[None]], axis=-1)
    return jnp.cos(ang), jnp.sin(ang)


def apply_rope(x, cos, sin):
    h = x.shape[-1] // 2
    x1 = x[..., :h].astype(jnp.float32)
    x2 = x[..., h:].astype(jnp.float32)
    return jnp.concatenate([x1 * cos - x2 * sin, x1 * sin + x2 * cos], axis=-1).astype(x.dtype)


def heads(t, n_heads, d):
    b, s, _ = t.shape
    return t.reshape(b, s, n_heads, d).transpose(0, 2, 1, 3)


def merge_heads(t):
    b, h, s, d = t.shape
    return t.transpose(0, 2, 1, 3).reshape(b, s, h * d)


def neighbourhood_attention(q, k, v, rpb):
    b, h, s, d = q.shape
    rows = s // GRID_W
    kh = min(NA_KH, rows)
    kw = NA_KW
    qg = q.reshape(b, h, rows, GRID_W, d)
    kg = k.reshape(b, h, rows, GRID_W, d)
    vg = v.reshape(b, h, rows, GRID_W, d)
    col = jnp.arange(GRID_W, dtype=jnp.int32)
    col_start = jnp.clip(col - kw // 2, 0, GRID_W - kw)
    col_idx = col_start[:, None] + jnp.arange(kw, dtype=jnp.int32)[None]
    col_off = col_idx - col[:, None]
    row_start = jnp.clip(jnp.arange(rows, dtype=jnp.int32) - kh // 2, 0, rows - kh)
    scale = d ** -0.5

    def one_row(args):
        r, rs = args
        qr = lax.dynamic_index_in_dim(qg, r, axis=2, keepdims=False)
        kb = lax.dynamic_slice_in_dim(kg, rs, kh, axis=2)
        vb = lax.dynamic_slice_in_dim(vg, rs, kh, axis=2)
        kwin = kb[:, :, :, col_idx]
        vwin = vb[:, :, :, col_idx]
        row_off = rs + jnp.arange(kh, dtype=jnp.int32) - r
        bias = rpb[:, row_off[:, None, None] + NA_KH - 1,
                   col_off[None] + NA_KW - 1]
        sc = jnp.einsum('bhwd,bhrwkd->bhwrk', qr, kwin).astype(jnp.float32) * scale
        sc = sc + bias.transpose(0, 2, 1, 3)[None].astype(jnp.float32)
        p = jax.nn.softmax(sc.reshape(b, h, GRID_W, kh * kw), axis=-1)
        p = p.reshape(b, h, GRID_W, kh, kw).astype(v.dtype)
        return jnp.einsum('bhwrk,bhrwkd->bhwd', p, vwin)

    out = lax.map(one_row, (jnp.arange(rows, dtype=jnp.int32), row_start))
    return out.transpose(1, 2, 0, 3, 4).reshape(b, h, s, d)


def block_attention(q, k, v, scale):
    b, hk, g, s, dk = q.shape
    dv = v.shape[-1]
    nb = s // Q_BLOCK
    qb = q.reshape(b, hk, g, nb, Q_BLOCK, dk).transpose(3, 0, 1, 2, 4, 5)

    def one_block(qi):
        sc = jnp.einsum('bkgqd,bksd->bkgqs', qi, k).astype(jnp.float32) * scale
        p = jax.nn.softmax(sc, axis=-1).astype(v.dtype)
        return jnp.einsum('bkgqs,bksd->bkgqd', p, v)

    out = lax.map(one_block, qb)
    return out.transpose(1, 2, 3, 0, 4, 5).reshape(b, hk, g, s, dv)


def layer_even(x, g_norm, w_in, rpb, qn_g, kn_g, w_out):
    b, s, _ = x.shape
    h = rms_norm(x, g_norm)
    proj = h @ w_in
    o1 = A_WIDTH
    o2 = 2 * A_WIDTH
    o3 = 3 * A_WIDTH
    o4 = o3 + B_WIDTH
    o5 = o4 + B_KV_WIDTH
    o6 = o5 + B_KV_WIDTH
    qa, ka, va, qb, kb, vb, gate = jnp.split(proj, [o1, o2, o3, o4, o5, o6], axis=-1)
    out_a = neighbourhood_attention(heads(qa, A_HEADS, HEAD_DIM), heads(ka, A_HEADS, HEAD_DIM),
                                    heads(va, A_HEADS, HEAD_DIM), rpb)
    cos, sin = axial_angles(s, HEAD_DIM)
    qh = apply_rope(rms_norm(heads(qb, B_HEADS, HEAD_DIM), qn_g), cos, sin)
    kh = apply_rope(rms_norm(heads(kb, B_KV_HEADS, HEAD_DIM), kn_g), cos, sin)
    vh = heads(vb, B_KV_HEADS, HEAD_DIM)
    qh = qh.reshape(b, B_KV_HEADS, B_GROUP, s, HEAD_DIM)
    out_b = block_attention(qh, kh, vh, HEAD_DIM ** -0.5).reshape(b, B_HEADS, s, HEAD_DIM)
    mixed = jnp.concatenate([merge_heads(out_a), merge_heads(out_b)], axis=-1)
    return x + (mixed * jax.nn.silu(gate)) @ w_out


def layer_odd(x, g_norm, w_in, qlat_g, kvlat_g, w_uq, w_ukv, w_out):
    b, s, _ = x.shape
    h = rms_norm(x, g_norm)
    proj = h @ w_in
    cq, ckv, k_rope, gate = jnp.split(
        proj, [C_Q_RANK, C_Q_RANK + C_KV_RANK, C_Q_RANK + C_KV_RANK + C_ROPE], axis=-1)
    q = heads(rms_norm(cq, qlat_g) @ w_uq, C_HEADS, C_QK_DIM)
    kv = heads(rms_norm(ckv, kvlat_g) @ w_ukv, C_HEADS, C_NOPE + C_V)
    q_nope, q_rope = q[..., :C_NOPE], q[..., C_NOPE:]
    k_nope, v = kv[..., :C_NOPE], kv[..., C_NOPE:]
    cos, sin = axial_angles(s, C_ROPE)
    q_rope = apply_rope(q_rope, cos, sin)
    k_r = apply_rope(k_rope[:, None], cos, sin)
    k = jnp.concatenate([k_nope, jnp.broadcast_to(k_r, (b, C_HEADS, s, C_ROPE))], axis=-1)
    qf = jnp.concatenate([q_nope, q_rope], axis=-1)[:, :, None]
    out = block_attention(qf, k, v, C_QK_DIM ** -0.5)[:, :, 0]
    mixed = merge_heads(out)
    return x + (mixed * jax.nn.silu(gate)) @ w_out


def trunk(x, norm_e, w_in_e, rpb_a, qnorm_b, knorm_b, w_out_e,
          norm_o, w_in_o, qlat_g, kvlat_g, w_uq, w_ukv, w_out_o, norm_f):
    for layer in range(DEPTH):
        i = layer // 2
        if layer % 2 == 0:
            x = layer_even(x, norm_e[i], w_in_e[i], rpb_a[i], qnorm_b[i], knorm_b[i], w_out_e[i])
        else:
            x = layer_odd(x, norm_o[i], w_in_o[i], qlat_g[i], kvlat_g[i], w_uq[i], w_ukv[i],
                          w_out_o[i])
    return rms_norm(x, norm_f)


def setup_inputs(seed: int = 0) -> dict:
    key = jax.random.key(seed)
    ks = jax.random.split(key, 18)

    def nrm(k, shape, fan_in):
        return jax.random.normal(k, shape, jnp.float32) * (fan_in ** -0.5)

    def gain(k, shape):
        return 1.0 + 0.01 * jax.random.normal(k, shape, jnp.float32)

    return {
        "x_prompt": jax.random.normal(ks[0], (BATCH, SEQ, D_MODEL), jnp.float32),
        "x_sample": jax.random.normal(ks[1], (DEC_BATCH, DEC_SEQ, D_MODEL), jnp.float32),
        "norm_e": gain(ks[2], (N_EVEN, D_MODEL)),
        "w_in_e": nrm(ks[3], (N_EVEN, D_MODEL, IN0_WIDTH), D_MODEL),
        "rpb_a": 0.1 * jax.random.normal(ks[4], (N_EVEN, A_HEADS, 2 * NA_KH - 1, 2 * NA_KW - 1),
                                         jnp.float32),
        "qnorm_b": gain(ks[5], (N_EVEN, HEAD_DIM)),
        "knorm_b": gain(ks[6], (N_EVEN, HEAD_DIM)),
        "w_out_e": nrm(ks[7], (N_EVEN, MIX0_WIDTH, D_MODEL), MIX0_WIDTH),
        "norm_o": gain(ks[8], (N_ODD, D_MODEL)),
        "w_in_o": nrm(ks[9], (N_ODD, D_MODEL, IN1_WIDTH), D_MODEL),
        "qlat_g": gain(ks[10], (N_ODD, C_Q_RANK)),
        "kvlat_g": gain(ks[11], (N_ODD, C_KV_RANK)),
        "w_uq": nrm(ks[12], (N_ODD, C_Q_RANK, C_HEADS * C_QK_DIM), C_Q_RANK),
        "w_ukv": nrm(ks[13], (N_ODD, C_KV_RANK, C_HEADS * (C_NOPE + C_V)), C_KV_RANK),
        "w_out_o": nrm(ks[14], (N_ODD, C_WIDTH, D_MODEL), C_WIDTH),
        "norm_f": gain(ks[15], (D_MODEL,)),
    }


def reference(x_prompt, x_sample, norm_e, w_in_e, rpb_a, qnorm_b, knorm_b, w_out_e,
              norm_o, w_in_o, qlat_g, kvlat_g, w_uq, w_ukv, w_out_o, norm_f):
    y_prompt = trunk(x_prompt, norm_e, w_in_e, rpb_a, qnorm_b, knorm_b, w_out_e,
                     norm_o, w_in_o, qlat_g, kvlat_g, w_uq, w_ukv, w_out_o, norm_f)
    y_sample = trunk(x_sample, norm_e, w_in_e, rpb_a, qnorm_b, knorm_b, w_out_e,
                     norm_o, w_in_o, qlat_g, kvlat_g, w_uq, w_ukv, w_out_o, norm_f)
    return (y_prompt, y_sample)
```

```python
import functools

import numpy as np
import jax
import jax.numpy as jnp
from jax import lax
from jax.experimental import pallas as pl
from jax.experimental.pallas import tpu as pltpu

D_MODEL = 1024
GRID_W = 64
HEAD_DIM = 64
A_HEADS = 8
NA_KH = 8
NA_KW = 16
B_HEADS = 8
B_KV_HEADS = 2
C_HEADS = 16
C_NOPE = 64
C_ROPE = 32
C_V = 64
C_Q_RANK = 384
C_KV_RANK = 256
ROPE_THETA = 10000.0
EPS = 1e-6

A_WIDTH = A_HEADS * HEAD_DIM
B_WIDTH = B_HEADS * HEAD_DIM
B_KV_WIDTH = B_KV_HEADS * HEAD_DIM
C_WIDTH = C_HEADS * C_V
C_QK_DIM = C_NOPE + C_ROPE

LANES = 128
TOKEN_TILE = 512
V_ROWS = HEAD_DIM + 16
NA_ROWS_PER_STEP = 8
GQA_Q_TILE = 128
MLA_Q_TILE = 256
VMEM_LIMIT = 56 * 1024 * 1024
MASKED = -1e30

F32 = jnp.float32
BF16 = jnp.bfloat16
_NT = (((1,), (1,)), ((), ()))


def _params(n_axes):
    return pltpu.CompilerParams(dimension_semantics=("arbitrary",) * n_axes,
                                vmem_limit_bytes=VMEM_LIMIT)


def _rms(x, g):
    return x * lax.rsqrt(jnp.mean(x * x, axis=-1, keepdims=True) + EPS) * g


def _silu(g):
    return g * jax.nn.sigmoid(g)


def _proj0_kernel(x_ref, g_ref, w_ref, cos_ref, sin_ref, qg_ref, kg_ref,
                  qa_ref, ka_ref, va_ref, qb_ref, kb_ref, vt_ref, gate_ref):
    tm = x_ref.shape[1]
    h = _rms(x_ref[0], g_ref[...]).astype(BF16)

    a = jnp.dot(h, w_ref[:, 0:3 * A_WIDTH], preferred_element_type=F32)
    qa_ref[0] = (a[:, 0:A_WIDTH] * HEAD_DIM ** -0.5).astype(BF16)
    ka_ref[0] = a[:, A_WIDTH:2 * A_WIDTH].astype(BF16)
    va_ref[0] = a[:, 2 * A_WIDTH:3 * A_WIDTH].astype(BF16)

    o = 3 * A_WIDTH
    b = jnp.dot(h, w_ref[:, o:o + B_WIDTH + 2 * B_KV_WIDTH], preferred_element_type=F32)
    cos = cos_ref[...]
    sin = sin_ref[...]
    lane = lax.broadcasted_iota(jnp.int32, (tm, LANES), 1)
    first = lane < HEAD_DIM
    low = (lane % HEAD_DIM) < HEAD_DIM // 2

    def norm_rope(xc, g):
        sq = xc * xc
        sa = jnp.sum(jnp.where(first, sq, 0.0), axis=-1, keepdims=True)
        sb = jnp.sum(jnp.where(first, 0.0, sq), axis=-1, keepdims=True)
        ms = jnp.where(first, sa, sb) * (1.0 / HEAD_DIM)
        y = xc * lax.rsqrt(ms + EPS) * g
        partner = jnp.where(low, pltpu.roll(y, LANES - HEAD_DIM // 2, 1),
                            pltpu.roll(y, HEAD_DIM // 2, 1))
        return y * cos + partner * sin

    qg = qg_ref[...]
    for j in range(B_WIDTH // LANES):
        qj = norm_rope(b[:, j * LANES:(j + 1) * LANES], qg)
        qb_ref[0, :, j * LANES:(j + 1) * LANES] = (qj * HEAD_DIM ** -0.5).astype(BF16)
    kk = norm_rope(b[:, B_WIDTH:B_WIDTH + LANES], kg_ref[...])
    ks = pltpu.roll(kk, HEAD_DIM, 1)
    kb_ref[0, 0] = jnp.where(first, kk, ks).astype(BF16)
    kb_ref[0, 1] = jnp.where(first, ks, kk).astype(BF16)
    vt = b[:, B_WIDTH + LANES:B_WIDTH + 2 * LANES].T
    ones = jnp.ones((V_ROWS - HEAD_DIM, tm), BF16)
    for kv in range(B_KV_HEADS):
        vt_ref[0, kv, 0, 0:HEAD_DIM, :] = vt[kv * HEAD_DIM:(kv + 1) * HEAD_DIM].astype(BF16)
        vt_ref[0, kv, 0, HEAD_DIM:V_ROWS, :] = ones

    o += B_WIDTH + 2 * B_KV_WIDTH
    gate_ref[0] = jnp.dot(h, w_ref[:, o:o + D_MODEL], preferred_element_type=F32).astype(BF16)


def _proj0(x, g, w, cos, sin, qg, kg):
    bsz, s, _ = x.shape
    tm = TOKEN_TILE
    nck = s // tm
    tok = lambda width: pl.BlockSpec((1, tm, width), lambda b, i: (b, i, 0))
    const = lambda shape: pl.BlockSpec(shape, lambda b, i: (0,) * len(shape))
    return pl.pallas_call(
        _proj0_kernel,
        grid=(bsz, nck),
        in_specs=[tok(D_MODEL), const((1, D_MODEL)), const(w.shape),
                  pl.BlockSpec((tm, LANES), lambda b, i: (i, 0)),
                  pl.BlockSpec((tm, LANES), lambda b, i: (i, 0)),
                  const((1, LANES)), const((1, LANES))],
        out_specs=[tok(A_WIDTH), tok(A_WIDTH), tok(A_WIDTH), tok(B_WIDTH),
                   pl.BlockSpec((1, B_KV_HEADS, tm, LANES), lambda b, i: (b, 0, i, 0)),
                   pl.BlockSpec((1, B_KV_HEADS, 1, V_ROWS, tm), lambda b, i: (b, 0, i, 0, 0)),
                   tok(D_MODEL)],
        out_shape=[jax.ShapeDtypeStruct((bsz, s, A_WIDTH), BF16)] * 3
        + [jax.ShapeDtypeStruct((bsz, s, B_WIDTH), BF16),
           jax.ShapeDtypeStruct((bsz, B_KV_HEADS, s, LANES), BF16),
           jax.ShapeDtypeStruct((bsz, B_KV_HEADS, nck, V_ROWS, tm), BF16),
           jax.ShapeDtypeStruct((bsz, s, D_MODEL), BF16)],
        compiler_params=_params(2),
        name="proj0",
    )(x, g, w, cos, sin, qg, kg)


def _na_kernel(q_ref, k_ref, v_ref, bias_ref, o_ref, *, rows):
    t = pl.program_id(2)
    lane = lax.broadcasted_iota(jnp.int32, (GRID_W, LANES), 1)
    first = lane < HEAD_DIM
    win = NA_KH * GRID_W

    def body(i, carry):
        r = t * NA_ROWS_PER_STEP + i
        rs = jnp.clip(r - NA_KH // 2, 0, rows - NA_KH)
        case = r - rs
        q = q_ref[0, pl.ds(pl.multiple_of(i * GRID_W, GRID_W), GRID_W), :]
        koff = pl.multiple_of(rs * GRID_W, GRID_W)
        kw = k_ref[0, pl.ds(koff, win), :]
        vw = v_ref[0, pl.ds(koff, win), :]
        outs = []
        for hh in range(2):
            qm = jnp.where(first if hh == 0 else jnp.logical_not(first), q, jnp.zeros_like(q))
            sc = lax.dot_general(qm, kw, _NT, preferred_element_type=F32) + bias_ref[hh, case]
            m = jnp.max(sc, axis=-1, keepdims=True)
            e = jnp.exp(sc - m)
            l = jnp.sum(e, axis=-1, keepdims=True)
            outs.append(jnp.dot(e.astype(BF16), vw, preferred_element_type=F32) / l)
        o_ref[0, pl.ds(pl.multiple_of(i * GRID_W, GRID_W), GRID_W), :] = (
            jnp.where(first, outs[0], outs[1]).astype(BF16))
        return carry

    lax.fori_loop(0, NA_ROWS_PER_STEP, body, 0)


def _na_bias_table(rpb):
    case = np.arange(NA_KH)[:, None, None, None]
    qc = np.arange(GRID_W)[None, :, None, None]
    wr = np.arange(NA_KH)[None, None, :, None]
    kc = np.arange(GRID_W)[None, None, None, :]
    cs = np.clip(qc - NA_KW // 2, 0, GRID_W - NA_KW)
    valid = np.broadcast_to((kc >= cs) & (kc < cs + NA_KW), (NA_KH, GRID_W, NA_KH, GRID_W))
    ridx = np.broadcast_to(wr - case + NA_KH - 1, valid.shape)
    cidx = np.broadcast_to(np.clip(kc - qc + NA_KW - 1, 0, 2 * NA_KW - 2), valid.shape)
    tbl = jnp.where(valid[None], rpb[:, ridx, cidx], MASKED)
    return tbl.reshape(rpb.shape[0], NA_KH, GRID_W, NA_KH * GRID_W).astype(F32)


def _na(qa, ka, va, bias):
    bsz, s, _ = qa.shape
    rows = s // GRID_W
    assert rows >= NA_KH and rows % NA_ROWS_PER_STEP == 0
    tq = NA_ROWS_PER_STEP * GRID_W
    return pl.pallas_call(
        functools.partial(_na_kernel, rows=rows),
        grid=(bsz, A_WIDTH // LANES, s // tq),
        in_specs=[pl.BlockSpec((1, tq, LANES), lambda b, hp, t: (b, t, hp)),
                  pl.BlockSpec((1, s, LANES), lambda b, hp, t: (b, 0, hp)),
                  pl.BlockSpec((1, s, LANES), lambda b, hp, t: (b, 0, hp)),
                  pl.BlockSpec((2, NA_KH, GRID_W, NA_KH * GRID_W), lambda b, hp, t: (hp, 0, 0, 0))],
        out_specs=pl.BlockSpec((1, tq, LANES), lambda b, hp, t: (b, t, hp)),
        out_shape=jax.ShapeDtypeStruct((bsz, s, A_WIDTH), BF16),
        compiler_params=_params(3),
        name="na",
    )(qa, ka, va, bias)


def _attend(q, k_ref, vt_ref):
    n = q.shape[0]
    nck, _, tk = vt_ref.shape

    def body(j, carry):
        m, acc = carry
        kc = k_ref[pl.ds(pl.multiple_of(j * tk, tk), tk), :]
        st = lax.dot_general(kc, q, _NT, preferred_element_type=F32)
        m_new = jnp.maximum(m, jnp.max(st, axis=0, keepdims=True))
        alpha = jnp.exp(m - m_new)
        p = jnp.exp(st - m_new).astype(BF16)
        pv = jnp.dot(vt_ref[j], p, preferred_element_type=F32)
        return m_new, acc * alpha + pv

    m0 = jnp.full((1, n), -jnp.inf, F32)
    acc0 = jnp.zeros((V_ROWS, n), F32)
    _, acc = lax.fori_loop(0, nck, body, (m0, acc0))
    return acc[0:HEAD_DIM] / acc[HEAD_DIM:HEAD_DIM + 1]


def _gqa_kernel(q_ref, k_ref, vt_ref, o_ref):
    tq = q_ref.shape[1]
    q = q_ref[0]
    first = lax.broadcasted_iota(jnp.int32, (tq, LANES), 1) < HEAD_DIM
    zero = jnp.zeros((tq, LANES), BF16)
    parts = []
    for c in range(q.shape[1] // LANES):
        qc = q[:, c * LANES:(c + 1) * LANES]
        parts += [jnp.where(first, qc, zero), jnp.where(first, zero, qc)]
    o = _attend(jnp.concatenate(parts, axis=0), k_ref.at[0, 0], vt_ref.at[0, 0])
    ot = jnp.concatenate([o[:, h * tq:(h + 1) * tq] for h in range(len(parts))], axis=0)
    o_ref[0] = ot.T.astype(BF16)


def _gqa(qb, kb, vt):
    bsz, s, _ = qb.shape
    tq = GQA_Q_TILE
    gw = B_WIDTH // B_KV_HEADS
    return pl.pallas_call(
        _gqa_kernel,
        grid=(bsz, B_KV_HEADS, s // tq),
        in_specs=[pl.BlockSpec((1, tq, gw), lambda b, kv, i: (b, i, kv)),
                  pl.BlockSpec((1, 1, s, LANES), lambda b, kv, i: (b, kv, 0, 0)),
                  pl.BlockSpec((1, 1) + vt.shape[2:], lambda b, kv, i: (b, kv, 0, 0, 0))],
        out_specs=pl.BlockSpec((1, tq, gw), lambda b, kv, i: (b, i, kv)),
        out_shape=jax.ShapeDtypeStruct((bsz, s, B_WIDTH), BF16),
        compiler_params=_params(3),
        name="gqa",
    )(qb, kb, vt)


def _mla_kernel(q_ref, k_ref, vt_ref, o_ref):
    outs = [_attend(q_ref[0, hh], k_ref.at[0, hh], vt_ref.at[0, hh]) for hh in range(2)]
    o_ref[0] = jnp.concatenate(outs, axis=0).T.astype(BF16)


def _mla(q, k, vt):
    bsz, nh, s, _ = q.shape
    tq = MLA_Q_TILE
    return pl.pallas_call(
        _mla_kernel,
        grid=(bsz, nh // 2, s // tq),
        in_specs=[pl.BlockSpec((1, 2, tq, LANES), lambda b, hp, i: (b, hp, i, 0)),
                  pl.BlockSpec((1, 2, s, LANES), lambda b, hp, i: (b, hp, 0, 0)),
                  pl.BlockSpec((1, 2) + vt.shape[2:], lambda b, hp, i: (b, hp, 0, 0, 0))],
        out_specs=pl.BlockSpec((1, tq, LANES), lambda b, hp, i: (b, i, hp)),
        out_shape=jax.ShapeDtypeStruct((bsz, s, nh * C_V), BF16),
        compiler_params=_params(3),
        name="mla",
    )(q, k, vt)


def _out0_kernel(x_ref, ma_ref, mb_ref, gate_ref, w_ref, y_ref):
    sg = _silu(gate_ref[0].astype(F32))
    ga = (ma_ref[0].astype(F32) * sg[:, 0:A_WIDTH]).astype(BF16)
    gb = (mb_ref[0].astype(F32) * sg[:, A_WIDTH:]).astype(BF16)
    y_ref[0] = (x_ref[0] + jnp.dot(ga, w_ref[0:A_WIDTH, :], preferred_element_type=F32)
                + jnp.dot(gb, w_ref[A_WIDTH:, :], preferred_element_type=F32))


def _out0(x, ma, mb, gate, w):
    bsz, s, _ = x.shape
    tm = TOKEN_TILE
    tok = lambda width: pl.BlockSpec((1, tm, width), lambda b, i: (b, i, 0))
    return pl.pallas_call(
        _out0_kernel,
        grid=(bsz, s // tm),
        in_specs=[tok(D_MODEL), tok(A_WIDTH), tok(B_WIDTH), tok(D_MODEL),
                  pl.BlockSpec(w.shape, lambda b, i: (0, 0))],
        out_specs=tok(D_MODEL),
        out_shape=jax.ShapeDtypeStruct(x.shape, F32),
        compiler_params=_params(2),
        name="out0",
    )(x, ma, mb, gate, w)


def _out1_kernel(x_ref, m_ref, gate_ref, w_ref, g_ref, y_ref):
    gm = (m_ref[0].astype(F32) * _silu(gate_ref[0].astype(F32))).astype(BF16)
    y = x_ref[0] + jnp.dot(gm, w_ref[...], preferred_element_type=F32)
    y_ref[0] = _rms(y, g_ref[...])


def _out1(x, m, gate, w, g):
    bsz, s, _ = x.shape
    tm = TOKEN_TILE
    tok = pl.BlockSpec((1, tm, D_MODEL), lambda b, i: (b, i, 0))
    return pl.pallas_call(
        _out1_kernel,
        grid=(bsz, s // tm),
        in_specs=[tok, tok, tok, pl.BlockSpec(w.shape, lambda b, i: (0, 0)),
                  pl.BlockSpec((1, D_MODEL), lambda b, i: (0, 0))],
        out_specs=tok,
        out_shape=jax.ShapeDtypeStruct(x.shape, F32),
        compiler_params=_params(2),
        name="out1",
    )(x, m, gate, w, g)


def _proj1_kernel(x_ref, g_ref, w_ref, qg_ref, kvg_ref, wuq_ref, wuk_ref, wuv_ref, cos_ref, sin_ref,
                  q_ref, k_ref, vt_ref, gate_ref):
    tm = x_ref.shape[1]
    h = _rms(x_ref[0], g_ref[...]).astype(BF16)
    lat_w = C_Q_RANK + C_KV_RANK + LANES
    lat = jnp.dot(h, w_ref[:, 0:lat_w], preferred_element_type=F32)
    gate_ref[0] = jnp.dot(h, w_ref[:, lat_w:], preferred_element_type=F32).astype(BF16)

    cos = cos_ref[...]
    sin = sin_ref[...]
    lane = lax.broadcasted_iota(jnp.int32, (tm, LANES), 1)
    low = lane < C_NOPE + C_ROPE // 2

    def rope(xc):
        partner = jnp.where(low, pltpu.roll(xc, LANES - C_ROPE // 2, 1),
                            pltpu.roll(xc, C_ROPE // 2, 1))
        return xc * cos + partner * sin

    cq = _rms(lat[:, 0:C_Q_RANK], qg_ref[...]).astype(BF16)
    q_all = jnp.dot(cq, wuq_ref[...], preferred_element_type=F32)
    for hd in range(C_HEADS):
        qh = rope(q_all[:, hd * LANES:(hd + 1) * LANES])
        q_ref[0, hd] = (qh * C_QK_DIM ** -0.5).astype(BF16)

    ckv = _rms(lat[:, C_Q_RANK:C_Q_RANK + C_KV_RANK], kvg_ref[...]).astype(BF16)
    k_rope = rope(lat[:, C_Q_RANK + C_KV_RANK:lat_w])
    k_all = jnp.dot(ckv, wuk_ref[...], preferred_element_type=F32)
    for hd in range(C_HEADS):
        k_ref[0, hd] = (k_all[:, hd * LANES:(hd + 1) * LANES] + k_rope).astype(BF16)

    vt = jnp.dot(ckv, wuv_ref[...], preferred_element_type=F32).T
    ones = jnp.ones((V_ROWS - C_V, tm), BF16)
    for hd in range(C_HEADS):
        vt_ref[0, hd, 0, 0:C_V, :] = vt[hd * C_V:(hd + 1) * C_V].astype(BF16)
        vt_ref[0, hd, 0, C_V:V_ROWS, :] = ones


def _proj1(x, g, w, qg, kvg, wuq, wuk, wuv, cos, sin):
    bsz, s, _ = x.shape
    tm = TOKEN_TILE
    nck = s // tm
    const = lambda shape: pl.BlockSpec(shape, lambda b, i: (0,) * len(shape))
    return pl.pallas_call(
        _proj1_kernel,
        grid=(bsz, nck),
        in_specs=[pl.BlockSpec((1, tm, D_MODEL), lambda b, i: (b, i, 0)),
                  const((1, D_MODEL)), const(w.shape), const((1, C_Q_RANK)), const((1, C_KV_RANK)),
                  const(wuq.shape), const(wuk.shape), const(wuv.shape),
                  pl.BlockSpec((tm, LANES), lambda b, i: (i, 0)),
                  pl.BlockSpec((tm, LANES), lambda b, i: (i, 0))],
        out_specs=[pl.BlockSpec((1, C_HEADS, tm, LANES), lambda b, i: (b, 0, i, 0)),
                   pl.BlockSpec((1, C_HEADS, tm, LANES), lambda b, i: (b, 0, i, 0)),
                   pl.BlockSpec((1, C_HEADS, 1, V_ROWS, tm), lambda b, i: (b, 0, i, 0, 0)),
                   pl.BlockSpec((1, tm, D_MODEL), lambda b, i: (b, i, 0))],
        out_shape=[jax.ShapeDtypeStruct((bsz, C_HEADS, s, LANES), BF16),
                   jax.ShapeDtypeStruct((bsz, C_HEADS, s, LANES), BF16),
                   jax.ShapeDtypeStruct((bsz, C_HEADS, nck, V_ROWS, tm), BF16),
                   jax.ShapeDtypeStruct((bsz, s, D_MODEL), BF16)],
        compiler_params=_params(2),
        name="proj1",
    )(x, g, w, qg, kvg, wuq, wuk, wuv, cos, sin)


def _axial_angles(n_tok, rot_dim):
    n_freq = rot_dim // 4
    inv = ROPE_THETA ** (-jnp.arange(n_freq, dtype=F32) / n_freq)
    t = jnp.arange(n_tok, dtype=jnp.int32)
    row = (t // GRID_W).astype(F32)
    col = (t % GRID_W).astype(F32)
    ang = jnp.concatenate([row[:, None] * inv[None], col[:, None] * inv[None]], axis=-1)
    return jnp.cos(ang), jnp.sin(ang)


def _rope_tables(s):
    c, sn = _axial_angles(s, HEAD_DIM)
    cos0 = jnp.tile(jnp.concatenate([c, c], axis=-1), (1, LANES // HEAD_DIM))
    sin0 = jnp.tile(jnp.concatenate([-sn, sn], axis=-1), (1, LANES // HEAD_DIM))
    c, sn = _axial_angles(s, C_ROPE)
    pad = LANES - C_QK_DIM
    cos1 = jnp.concatenate([jnp.ones((s, C_NOPE), F32), c, c, jnp.ones((s, pad), F32)], axis=-1)
    sin1 = jnp.concatenate([jnp.zeros((s, C_NOPE), F32), -sn, sn, jnp.zeros((s, pad), F32)], axis=-1)
    return cos0, sin0, cos1, sin1


def _prepare(norm_e, w_in_e, rpb_a, qnorm_b, knorm_b, w_out_e,
             norm_o, w_in_o, qlat_g, kvlat_g, w_uq, w_ukv, w_out_o, norm_f):
    rep = LANES // HEAD_DIM
    lat = C_Q_RANK + C_KV_RANK
    z = lambda n: jnp.zeros((D_MODEL, n), F32)
    w1 = jnp.concatenate([w_in_o[0][:, :lat], z(C_NOPE), w_in_o[0][:, lat:lat + C_ROPE],
                          z(LANES - C_QK_DIM), w_in_o[0][:, lat + C_ROPE:]], axis=1)
    wuq = jnp.pad(w_uq[0].reshape(C_Q_RANK, C_HEADS, C_QK_DIM),
                  ((0, 0), (0, 0), (0, LANES - C_QK_DIM))).reshape(C_Q_RANK, C_HEADS * LANES)
    wkv = w_ukv[0].reshape(C_KV_RANK, C_HEADS, C_NOPE + C_V)
    wuk = jnp.pad(wkv[:, :, :C_NOPE], ((0, 0), (0, 0), (0, LANES - C_NOPE))).reshape(C_KV_RANK, C_HEADS * LANES)
    wuv = wkv[:, :, C_NOPE:].reshape(C_KV_RANK, C_WIDTH)
    return dict(
        norm_e=norm_e[0][None], w_in_e=w_in_e[0].astype(BF16), bias=_na_bias_table(rpb_a[0]),
        qg=jnp.tile(qnorm_b[0], rep)[None], kg=jnp.tile(knorm_b[0], rep)[None],
        w_out_e=w_out_e[0].astype(BF16),
        norm_o=norm_o[0][None], w1=w1.astype(BF16), qlat_g=qlat_g[0][None], kvlat_g=kvlat_g[0][None],
        wuq=wuq.astype(BF16), wuk=wuk.astype(BF16), wuv=wuv.astype(BF16),
        w_out_o=w_out_o[0].astype(BF16), norm_f=norm_f[None])


def _trunk(x, p):
    cos0, sin0, cos1, sin1 = _rope_tables(x.shape[1])
    qa, ka, va, qb, kb, vbt, gate0 = _proj0(x, p["norm_e"], p["w_in_e"], cos0, sin0, p["qg"], p["kg"])
    mix_a = _na(qa, ka, va, p["bias"])
    mix_b = _gqa(qb, kb, vbt)
    x1 = _out0(x, mix_a, mix_b, gate0, p["w_out_e"])
    q, k, vt, gate1 = _proj1(x1, p["norm_o"], p["w1"], p["qlat_g"], p["kvlat_g"],
                             p["wuq"], p["wuk"], p["wuv"], cos1, sin1)
    mix_c = _mla(q, k, vt)
    return _out1(x1, mix_c, gate1, p["w_out_o"], p["norm_f"])


def kernel(x_prompt, x_sample, norm_e, w_in_e, rpb_a, qnorm_b, knorm_b, w_out_e,
           norm_o, w_in_o, qlat_g, kvlat_g, w_uq, w_ukv, w_out_o, norm_f):
    assert norm_e.shape[0] == 1 and norm_o.shape[0] == 1
    p = _prepare(norm_e, w_in_e, rpb_a, qnorm_b, knorm_b, w_out_e,
                 norm_o, w_in_o, qlat_g, kvlat_g, w_uq, w_ukv, w_out_o, norm_f)
    return (_trunk(x_prompt, p), _trunk(x_sample, p))
```

```python
import functools

import numpy as np
import jax
import jax.numpy as jnp
from jax import lax
from jax.experimental import pallas as pl
from jax.experimental.pallas import tpu as pltpu

D_MODEL = 1024
GRID_W = 64
HEAD_DIM = 64
A_HEADS = 8
NA_KH = 8
NA_KW = 16
B_HEADS = 8
B_KV_HEADS = 2
C_HEADS = 16
C_NOPE = 64
C_ROPE = 32
C_V = 64
C_Q_RANK = 384
C_KV_RANK = 256
ROPE_THETA = 10000.0
EPS = 1e-6

A_WIDTH = A_HEADS * HEAD_DIM
B_WIDTH = B_HEADS * HEAD_DIM
B_KV_WIDTH = B_KV_HEADS * HEAD_DIM
C_WIDTH = C_HEADS * C_V
C_QK_DIM = C_NOPE + C_ROPE

LANES = 128
TOKEN_TILE = 512
V_ROWS = HEAD_DIM + 16
NA_ROWS_PER_STEP = 8
STREAM_QUERIES = 256
GQA_Q_TILE = STREAM_QUERIES
MLA_Q_TILE = 2 * STREAM_QUERIES
LOG2E = 1.4426950408889634
VMEM_LIMIT = 56 * 1024 * 1024
MASKED = -1e30

F32 = jnp.float32
BF16 = jnp.bfloat16
_NT = (((1,), (1,)), ((), ()))


def _params(n_axes):
    return pltpu.CompilerParams(dimension_semantics=("arbitrary",) * n_axes,
                                vmem_limit_bytes=VMEM_LIMIT)


def _rms(x, g):
    return x * lax.rsqrt(jnp.mean(x * x, axis=-1, keepdims=True) + EPS) * g


def _silu(g):
    return g * jax.nn.sigmoid(g)


def _proj0_kernel(x_ref, g_ref, w_ref, cos_ref, sin_ref, qg_ref, kg_ref,
                  qa_ref, ka_ref, va_ref, qb_ref, kb_ref, vt_ref, gate_ref):
    tm = x_ref.shape[1]
    h = _rms(x_ref[0], g_ref[...]).astype(BF16)

    a = jnp.dot(h, w_ref[:, 0:3 * A_WIDTH], preferred_element_type=F32)
    qa_ref[0] = (a[:, 0:A_WIDTH] * HEAD_DIM ** -0.5).astype(BF16)
    ka_ref[0] = a[:, A_WIDTH:2 * A_WIDTH].astype(BF16)
    va_ref[0] = a[:, 2 * A_WIDTH:3 * A_WIDTH].astype(BF16)

    o = 3 * A_WIDTH
    b = jnp.dot(h, w_ref[:, o:o + B_WIDTH + 2 * B_KV_WIDTH], preferred_element_type=F32)
    cos = cos_ref[...]
    sin = sin_ref[...]
    lane = lax.broadcasted_iota(jnp.int32, (tm, LANES), 1)
    first = lane < HEAD_DIM
    low = (lane % HEAD_DIM) < HEAD_DIM // 2

    def norm_rope(xc, g):
        sq = xc * xc
        sa = jnp.sum(jnp.where(first, sq, 0.0), axis=-1, keepdims=True)
        sb = jnp.sum(jnp.where(first, 0.0, sq), axis=-1, keepdims=True)
        ms = jnp.where(first, sa, sb) * (1.0 / HEAD_DIM)
        y = xc * lax.rsqrt(ms + EPS) * g
        partner = jnp.where(low, pltpu.roll(y, LANES - HEAD_DIM // 2, 1),
                            pltpu.roll(y, HEAD_DIM // 2, 1))
        return y * cos + partner * sin

    qg = qg_ref[...]
    for j in range(B_WIDTH // LANES):
        qj = norm_rope(b[:, j * LANES:(j + 1) * LANES], qg)
        qb_ref[0, :, j * LANES:(j + 1) * LANES] = (qj * (HEAD_DIM ** -0.5 * LOG2E)).astype(BF16)
    kk = norm_rope(b[:, B_WIDTH:B_WIDTH + LANES], kg_ref[...])
    ks = pltpu.roll(kk, HEAD_DIM, 1)
    kb_ref[0, 0] = jnp.where(first, kk, ks).astype(BF16)
    kb_ref[0, 1] = jnp.where(first, ks, kk).astype(BF16)
    vt = b[:, B_WIDTH + LANES:B_WIDTH + 2 * LANES].T
    ones = jnp.ones((V_ROWS - HEAD_DIM, tm), BF16)
    for kv in range(B_KV_HEADS):
        vt_ref[0, kv, 0, 0:HEAD_DIM, :] = vt[kv * HEAD_DIM:(kv + 1) * HEAD_DIM].astype(BF16)
        vt_ref[0, kv, 0, HEAD_DIM:V_ROWS, :] = ones

    o += B_WIDTH + 2 * B_KV_WIDTH
    gate_ref[0] = jnp.dot(h, w_ref[:, o:o + D_MODEL], preferred_element_type=F32).astype(BF16)


def _proj0(x, g, w, cos, sin, qg, kg):
    bsz, s, _ = x.shape
    tm = TOKEN_TILE
    nck = s // tm
    tok = lambda width: pl.BlockSpec((1, tm, width), lambda b, i: (b, i, 0))
    const = lambda shape: pl.BlockSpec(shape, lambda b, i: (0,) * len(shape))
    return pl.pallas_call(
        _proj0_kernel,
        grid=(bsz, nck),
        in_specs=[tok(D_MODEL), const((1, D_MODEL)), const(w.shape),
                  pl.BlockSpec((tm, LANES), lambda b, i: (i, 0)),
                  pl.BlockSpec((tm, LANES), lambda b, i: (i, 0)),
                  const((1, LANES)), const((1, LANES))],
        out_specs=[tok(A_WIDTH), tok(A_WIDTH), tok(A_WIDTH), tok(B_WIDTH),
                   pl.BlockSpec((1, B_KV_HEADS, tm, LANES), lambda b, i: (b, 0, i, 0)),
                   pl.BlockSpec((1, B_KV_HEADS, 1, V_ROWS, tm), lambda b, i: (b, 0, i, 0, 0)),
                   tok(D_MODEL)],
        out_shape=[jax.ShapeDtypeStruct((bsz, s, A_WIDTH), BF16)] * 3
        + [jax.ShapeDtypeStruct((bsz, s, B_WIDTH), BF16),
           jax.ShapeDtypeStruct((bsz, B_KV_HEADS, s, LANES), BF16),
           jax.ShapeDtypeStruct((bsz, B_KV_HEADS, nck, V_ROWS, tm), BF16),
           jax.ShapeDtypeStruct((bsz, s, D_MODEL), BF16)],
        compiler_params=_params(2),
        name="proj0",
    )(x, g, w, cos, sin, qg, kg)


def _na_kernel(q_ref, k_ref, v_ref, bias_ref, o_ref, *, rows):
    t = pl.program_id(2)
    lane = lax.broadcasted_iota(jnp.int32, (GRID_W, LANES), 1)
    first = lane < HEAD_DIM
    win = NA_KH * GRID_W

    def body(i, carry):
        r = t * NA_ROWS_PER_STEP + i
        rs = jnp.clip(r - NA_KH // 2, 0, rows - NA_KH)
        case = r - rs
        q = q_ref[0, pl.ds(pl.multiple_of(i * GRID_W, GRID_W), GRID_W), :]
        koff = pl.multiple_of(rs * GRID_W, GRID_W)
        kw = k_ref[0, pl.ds(koff, win), :]
        vw = v_ref[0, pl.ds(koff, win), :]
        outs = []
        for hh in range(2):
            qm = jnp.where(first if hh == 0 else jnp.logical_not(first), q, jnp.zeros_like(q))
            sc = lax.dot_general(qm, kw, _NT, preferred_element_type=F32) + bias_ref[hh, case]
            m = jnp.max(sc, axis=-1, keepdims=True)
            e = jnp.exp(sc - m)
            l = jnp.sum(e, axis=-1, keepdims=True)
            outs.append(jnp.dot(e.astype(BF16), vw, preferred_element_type=F32) / l)
        o_ref[0, pl.ds(pl.multiple_of(i * GRID_W, GRID_W), GRID_W), :] = (
            jnp.where(first, outs[0], outs[1]).astype(BF16))
        return carry

    lax.fori_loop(0, NA_ROWS_PER_STEP, body, 0)


def _na_bias_table(rpb):
    case = np.arange(NA_KH)[:, None, None, None]
    qc = np.arange(GRID_W)[None, :, None, None]
    wr = np.arange(NA_KH)[None, None, :, None]
    kc = np.arange(GRID_W)[None, None, None, :]
    cs = np.clip(qc - NA_KW // 2, 0, GRID_W - NA_KW)
    valid = np.broadcast_to((kc >= cs) & (kc < cs + NA_KW), (NA_KH, GRID_W, NA_KH, GRID_W))
    ridx = np.broadcast_to(wr - case + NA_KH - 1, valid.shape)
    cidx = np.broadcast_to(np.clip(kc - qc + NA_KW - 1, 0, 2 * NA_KW - 2), valid.shape)
    tbl = jnp.where(valid[None], rpb[:, ridx, cidx], MASKED)
    return tbl.reshape(rpb.shape[0], NA_KH, GRID_W, NA_KH * GRID_W).astype(F32)


def _na(qa, ka, va, bias):
    bsz, s, _ = qa.shape
    rows = s // GRID_W
    assert rows >= NA_KH and rows % NA_ROWS_PER_STEP == 0
    tq = NA_ROWS_PER_STEP * GRID_W
    return pl.pallas_call(
        functools.partial(_na_kernel, rows=rows),
        grid=(bsz, A_WIDTH // LANES, s // tq),
        in_specs=[pl.BlockSpec((1, tq, LANES), lambda b, hp, t: (b, t, hp)),
                  pl.BlockSpec((1, s, LANES), lambda b, hp, t: (b, 0, hp)),
                  pl.BlockSpec((1, s, LANES), lambda b, hp, t: (b, 0, hp)),
                  pl.BlockSpec((2, NA_KH, GRID_W, NA_KH * GRID_W), lambda b, hp, t: (hp, 0, 0, 0))],
        out_specs=pl.BlockSpec((1, tq, LANES), lambda b, hp, t: (b, t, hp)),
        out_shape=jax.ShapeDtypeStruct((bsz, s, A_WIDTH), BF16),
        compiler_params=_params(3),
        name="na",
    )(qa, ka, va, bias)


def _attend(streams, st_ref):
    nck, _, tk = streams[0][2].shape

    def scores(j, dst):
        off = pl.multiple_of(j * tk, tk)
        for i, (q, k_ref, _) in enumerate(streams):
            dst[i] = lax.dot_general(k_ref[pl.ds(off, tk), :], q, _NT,
                                     preferred_element_type=F32)

    def step(j, cur, nxt, carry):
        sm = []
        for i, (m, _) in enumerate(carry):
            st = cur[i]
            m_new = jnp.maximum(m, jnp.max(st, axis=0, keepdims=True))
            sm.append((m_new, jnp.exp2(m - m_new), jnp.exp2(st - m_new).astype(BF16)))
        if nxt is not None:
            scores(j + 1, nxt)
        pvs = [jnp.dot(vt_ref[j], p, preferred_element_type=F32)
               for (_, _, vt_ref), (_, _, p) in zip(streams, sm)]
        return tuple((m_new, acc * alpha + pv)
                     for (m_new, alpha, _), pv, (_, acc) in zip(sm, pvs, carry))

    def body(jj, carry):
        carry = step(2 * jj, st_ref[0], st_ref[1], carry)
        return step(2 * jj + 1, st_ref[1], st_ref[0], carry)

    scores(0, st_ref[0])
    carry = tuple((jnp.full((1, q.shape[0]), -jnp.inf, F32), jnp.zeros((V_ROWS, q.shape[0]), F32))
                  for q, _, _ in streams)
    pairs = (nck - 1) // 2
    carry = lax.fori_loop(0, pairs, body, carry)
    for j in range(2 * pairs, nck):
        carry = step(j, st_ref[j % 2], st_ref[(j + 1) % 2] if j + 1 < nck else None, carry)
    return [acc[0:HEAD_DIM] / acc[HEAD_DIM:HEAD_DIM + 1] for _, acc in carry]


def _score_scratch(n_streams, tk, n):
    return [pltpu.VMEM((n_streams, tk, n), F32)] * 2


def _gqa_kernel(q_ref, k_ref, vt_ref, o_ref, *st_ref):
    tq = q_ref.shape[1]
    q = q_ref[0]
    first = lax.broadcasted_iota(jnp.int32, (tq, LANES), 1) < HEAD_DIM
    zero = jnp.zeros((tq, LANES), BF16)
    heads = []
    for c in range(q.shape[1] // LANES):
        qc = q[:, c * LANES:(c + 1) * LANES]
        heads += [jnp.where(first, qc, zero), jnp.where(first, zero, qc)]
    outs = _attend([(qh, k_ref.at[0, 0], vt_ref.at[0, 0]) for qh in heads], st_ref)
    o_ref[0] = jnp.concatenate(outs, axis=0).T.astype(BF16)


def _gqa(qb, kb, vt):
    bsz, s, _ = qb.shape
    tq = GQA_Q_TILE
    gw = B_WIDTH // B_KV_HEADS
    return pl.pallas_call(
        _gqa_kernel,
        grid=(bsz, B_KV_HEADS, s // tq),
        in_specs=[pl.BlockSpec((1, tq, gw), lambda b, kv, i: (b, i, kv)),
                  pl.BlockSpec((1, 1, s, LANES), lambda b, kv, i: (b, kv, 0, 0)),
                  pl.BlockSpec((1, 1) + vt.shape[2:], lambda b, kv, i: (b, kv, 0, 0, 0))],
        out_specs=pl.BlockSpec((1, tq, gw), lambda b, kv, i: (b, i, kv)),
        out_shape=jax.ShapeDtypeStruct((bsz, s, B_WIDTH), BF16),
        scratch_shapes=_score_scratch(gw // HEAD_DIM, vt.shape[-1], tq),
        compiler_params=_params(3),
        name="gqa",
    )(qb, kb, vt)


def _mla_kernel(q_ref, k_ref, vt_ref, o_ref, *st_ref):
    tq = q_ref.shape[2]
    sub = st_ref[0].shape[-1]
    streams = [(q_ref[0, hh, c * sub:(c + 1) * sub, :], k_ref.at[0, hh], vt_ref.at[0, hh])
               for hh in range(2) for c in range(tq // sub)]
    outs = _attend(streams, st_ref)
    per_head = len(outs) // 2
    ot = jnp.concatenate([jnp.concatenate(outs[hh * per_head:(hh + 1) * per_head], axis=1)
                          for hh in range(2)], axis=0)
    o_ref[0] = ot.T.astype(BF16)


def _mla(q, k, vt):
    bsz, nh, s, _ = q.shape
    tq = MLA_Q_TILE
    return pl.pallas_call(
        _mla_kernel,
        grid=(bsz, nh // 2, s // tq),
        in_specs=[pl.BlockSpec((1, 2, tq, LANES), lambda b, hp, i: (b, hp, i, 0)),
                  pl.BlockSpec((1, 2, s, LANES), lambda b, hp, i: (b, hp, 0, 0)),
                  pl.BlockSpec((1, 2) + vt.shape[2:], lambda b, hp, i: (b, hp, 0, 0, 0))],
        out_specs=pl.BlockSpec((1, tq, LANES), lambda b, hp, i: (b, i, hp)),
        out_shape=jax.ShapeDtypeStruct((bsz, s, nh * C_V), BF16),
        scratch_shapes=_score_scratch(2 * tq // STREAM_QUERIES, vt.shape[-1], STREAM_QUERIES),
        compiler_params=_params(3),
        name="mla",
    )(q, k, vt)


def _out0_kernel(x_ref, ma_ref, mb_ref, gate_ref, w_ref, y_ref):
    sg = _silu(gate_ref[0].astype(F32))
    ga = (ma_ref[0].astype(F32) * sg[:, 0:A_WIDTH]).astype(BF16)
    gb = (mb_ref[0].astype(F32) * sg[:, A_WIDTH:]).astype(BF16)
    y_ref[0] = (x_ref[0] + jnp.dot(ga, w_ref[0:A_WIDTH, :], preferred_element_type=F32)
                + jnp.dot(gb, w_ref[A_WIDTH:, :], preferred_element_type=F32))


def _out0(x, ma, mb, gate, w):
    bsz, s, _ = x.shape
    tm = TOKEN_TILE
    tok = lambda width: pl.BlockSpec((1, tm, width), lambda b, i: (b, i, 0))
    return pl.pallas_call(
        _out0_kernel,
        grid=(bsz, s // tm),
        in_specs=[tok(D_MODEL), tok(A_WIDTH), tok(B_WIDTH), tok(D_MODEL),
                  pl.BlockSpec(w.shape, lambda b, i: (0, 0))],
        out_specs=tok(D_MODEL),
        out_shape=jax.ShapeDtypeStruct(x.shape, F32),
        compiler_params=_params(2),
        name="out0",
    )(x, ma, mb, gate, w)


def _out1_kernel(x_ref, m_ref, gate_ref, w_ref, g_ref, y_ref):
    gm = (m_ref[0].astype(F32) * _silu(gate_ref[0].astype(F32))).astype(BF16)
    y = x_ref[0] + jnp.dot(gm, w_ref[...], preferred_element_type=F32)
    y_ref[0] = _rms(y, g_ref[...])


def _out1(x, m, gate, w, g):
    bsz, s, _ = x.shape
    tm = TOKEN_TILE
    tok = pl.BlockSpec((1, tm, D_MODEL), lambda b, i: (b, i, 0))
    return pl.pallas_call(
        _out1_kernel,
        grid=(bsz, s // tm),
        in_specs=[tok, tok, tok, pl.BlockSpec(w.shape, lambda b, i: (0, 0)),
                  pl.BlockSpec((1, D_MODEL), lambda b, i: (0, 0))],
        out_specs=tok,
        out_shape=jax.ShapeDtypeStruct(x.shape, F32),
        compiler_params=_params(2),
        name="out1",
    )(x, m, gate, w, g)


def _proj1_kernel(x_ref, g_ref, w_ref, qg_ref, kvg_ref, wuq_ref, wuk_ref, wuv_ref, cos_ref, sin_ref,
                  q_ref, k_ref, vt_ref, gate_ref):
    tm = x_ref.shape[1]
    h = _rms(x_ref[0], g_ref[...]).astype(BF16)
    lat_w = C_Q_RANK + C_KV_RANK + LANES
    lat = jnp.dot(h, w_ref[:, 0:lat_w], preferred_element_type=F32)
    gate_ref[0] = jnp.dot(h, w_ref[:, lat_w:], preferred_element_type=F32).astype(BF16)

    cos = cos_ref[...]
    sin = sin_ref[...]
    lane = lax.broadcasted_iota(jnp.int32, (tm, LANES), 1)
    low = lane < C_NOPE + C_ROPE // 2

    def rope(xc):
        partner = jnp.where(low, pltpu.roll(xc, LANES - C_ROPE // 2, 1),
                            pltpu.roll(xc, C_ROPE // 2, 1))
        return xc * cos + partner * sin

    cq = _rms(lat[:, 0:C_Q_RANK], qg_ref[...]).astype(BF16)
    q_all = jnp.dot(cq, wuq_ref[...], preferred_element_type=F32)
    for hd in range(C_HEADS):
        qh = rope(q_all[:, hd * LANES:(hd + 1) * LANES])
        q_ref[0, hd] = (qh * (C_QK_DIM ** -0.5 * LOG2E)).astype(BF16)

    ckv = _rms(lat[:, C_Q_RANK:C_Q_RANK + C_KV_RANK], kvg_ref[...]).astype(BF16)
    k_rope = rope(lat[:, C_Q_RANK + C_KV_RANK:lat_w])
    k_all = jnp.dot(ckv, wuk_ref[...], preferred_element_type=F32)
    for hd in range(C_HEADS):
        k_ref[0, hd] = (k_all[:, hd * LANES:(hd + 1) * LANES] + k_rope).astype(BF16)

    vt = jnp.dot(ckv, wuv_ref[...], preferred_element_type=F32).T
    ones = jnp.ones((V_ROWS - C_V, tm), BF16)
    for hd in range(C_HEADS):
        vt_ref[0, hd, 0, 0:C_V, :] = vt[hd * C_V:(hd + 1) * C_V].astype(BF16)
        vt_ref[0, hd, 0, C_V:V_ROWS, :] = ones


def _proj1(x, g, w, qg, kvg, wuq, wuk, wuv, cos, sin):
    bsz, s, _ = x.shape
    tm = TOKEN_TILE
    nck = s // tm
    const = lambda shape: pl.BlockSpec(shape, lambda b, i: (0,) * len(shape))
    return pl.pallas_call(
        _proj1_kernel,
        grid=(bsz, nck),
        in_specs=[pl.BlockSpec((1, tm, D_MODEL), lambda b, i: (b, i, 0)),
                  const((1, D_MODEL)), const(w.shape), const((1, C_Q_RANK)), const((1, C_KV_RANK)),
                  const(wuq.shape), const(wuk.shape), const(wuv.shape),
                  pl.BlockSpec((tm, LANES), lambda b, i: (i, 0)),
                  pl.BlockSpec((tm, LANES), lambda b, i: (i, 0))],
        out_specs=[pl.BlockSpec((1, C_HEADS, tm, LANES), lambda b, i: (b, 0, i, 0)),
                   pl.BlockSpec((1, C_HEADS, tm, LANES), lambda b, i: (b, 0, i, 0)),
                   pl.BlockSpec((1, C_HEADS, 1, V_ROWS, tm), lambda b, i: (b, 0, i, 0, 0)),
                   pl.BlockSpec((1, tm, D_MODEL), lambda b, i: (b, i, 0))],
        out_shape=[jax.ShapeDtypeStruct((bsz, C_HEADS, s, LANES), BF16),
                   jax.ShapeDtypeStruct((bsz, C_HEADS, s, LANES), BF16),
                   jax.ShapeDtypeStruct((bsz, C_HEADS, nck, V_ROWS, tm), BF16),
                   jax.ShapeDtypeStruct((bsz, s, D_MODEL), BF16)],
        compiler_params=_params(2),
        name="proj1",
    )(x, g, w, qg, kvg, wuq, wuk, wuv, cos, sin)


def _axial_angles(n_tok, rot_dim):
    n_freq = rot_dim // 4
    inv = ROPE_THETA ** (-jnp.arange(n_freq, dtype=F32) / n_freq)
    t = jnp.arange(n_tok, dtype=jnp.int32)
    row = (t // GRID_W).astype(F32)
    col = (t % GRID_W).astype(F32)
    ang = jnp.concatenate([row[:, None] * inv[None], col[:, None] * inv[None]], axis=-1)
    return jnp.cos(ang), jnp.sin(ang)


def _rope_tables(s):
    c, sn = _axial_angles(s, HEAD_DIM)
    cos0 = jnp.tile(jnp.concatenate([c, c], axis=-1), (1, LANES // HEAD_DIM))
    sin0 = jnp.tile(jnp.concatenate([-sn, sn], axis=-1), (1, LANES // HEAD_DIM))
    c, sn = _axial_angles(s, C_ROPE)
    pad = LANES - C_QK_DIM
    cos1 = jnp.concatenate([jnp.ones((s, C_NOPE), F32), c, c, jnp.ones((s, pad), F32)], axis=-1)
    sin1 = jnp.concatenate([jnp.zeros((s, C_NOPE), F32), -sn, sn, jnp.zeros((s, pad), F32)], axis=-1)
    return cos0, sin0, cos1, sin1


def _prepare(norm_e, w_in_e, rpb_a, qnorm_b, knorm_b, w_out_e,
             norm_o, w_in_o, qlat_g, kvlat_g, w_uq, w_ukv, w_out_o, norm_f):
    rep = LANES // HEAD_DIM
    lat = C_Q_RANK + C_KV_RANK
    z = lambda n: jnp.zeros((D_MODEL, n), F32)
    w1 = jnp.concatenate([w_in_o[0][:, :lat], z(C_NOPE), w_in_o[0][:, lat:lat + C_ROPE],
                          z(LANES - C_QK_DIM), w_in_o[0][:, lat + C_ROPE:]], axis=1)
    wuq = jnp.pad(w_uq[0].reshape(C_Q_RANK, C_HEADS, C_QK_DIM),
                  ((0, 0), (0, 0), (0, LANES - C_QK_DIM))).reshape(C_Q_RANK, C_HEADS * LANES)
    wkv = w_ukv[0].reshape(C_KV_RANK, C_HEADS, C_NOPE + C_V)
    wuk = jnp.pad(wkv[:, :, :C_NOPE], ((0, 0), (0, 0), (0, LANES - C_NOPE))).reshape(C_KV_RANK, C_HEADS * LANES)
    wuv = wkv[:, :, C_NOPE:].reshape(C_KV_RANK, C_WIDTH)
    return dict(
        norm_e=norm_e[0][None], w_in_e=w_in_e[0].astype(BF16), bias=_na_bias_table(rpb_a[0]),
        qg=jnp.tile(qnorm_b[0], rep)[None], kg=jnp.tile(knorm_b[0], rep)[None],
        w_out_e=w_out_e[0].astype(BF16),
        norm_o=norm_o[0][None], w1=w1.astype(BF16), qlat_g=qlat_g[0][None], kvlat_g=kvlat_g[0][None],
        wuq=wuq.astype(BF16), wuk=wuk.astype(BF16), wuv=wuv.astype(BF16),
        w_out_o=w_out_o[0].astype(BF16), norm_f=norm_f[None])


def _trunk(x, p):
    cos0, sin0, cos1, sin1 = _rope_tables(x.shape[1])
    qa, ka, va, qb, kb, vbt, gate0 = _proj0(x, p["norm_e"], p["w_in_e"], cos0, sin0, p["qg"], p["kg"])
    mix_a = _na(qa, ka, va, p["bias"])
    mix_b = _gqa(qb, kb, vbt)
    x1 = _out0(x, mix_a, mix_b, gate0, p["w_out_e"])
    q, k, vt, gate1 = _proj1(x1, p["norm_o"], p["w1"], p["qlat_g"], p["kvlat_g"],
                             p["wuq"], p["wuk"], p["wuv"], cos1, sin1)
    mix_c = _mla(q, k, vt)
    return _out1(x1, mix_c, gate1, p["w_out_o"], p["norm_f"])


def kernel(x_prompt, x_sample, norm_e, w_in_e, rpb_a, qnorm_b, knorm_b, w_out_e,
           norm_o, w_in_o, qlat_g, kvlat_g, w_uq, w_ukv, w_out_o, norm_f):
    assert norm_e.shape[0] == 1 and norm_o.shape[0] == 1
    p = _prepare(norm_e, w_in_e, rpb_a, qnorm_b, knorm_b, w_out_e,
                 norm_o, w_in_o, qlat_g, kvlat_g, w_uq, w_ukv, w_out_o, norm_f)
    return (_trunk(x_prompt, p), _trunk(x_sample, p))
```

```python
import functools

import numpy as np
import jax
import jax.numpy as jnp
from jax import lax
from jax.experimental import pallas as pl
from jax.experimental.pallas import tpu as pltpu

D_MODEL = 1024
GRID_W = 64
HEAD_DIM = 64
A_HEADS = 8
NA_KH = 8
NA_KW = 16
B_HEADS = 8
B_KV_HEADS = 2
C_HEADS = 16
C_NOPE = 64
C_ROPE = 32
C_V = 64
C_Q_RANK = 384
C_KV_RANK = 256
ROPE_THETA = 10000.0
EPS = 1e-6

A_WIDTH = A_HEADS * HEAD_DIM
B_WIDTH = B_HEADS * HEAD_DIM
B_KV_WIDTH = B_KV_HEADS * HEAD_DIM
C_WIDTH = C_HEADS * C_V
C_QK_DIM = C_NOPE + C_ROPE

LANES = 128
TOKEN_TILE = 512
V_ROWS = HEAD_DIM + 16
NA_ROWS_PER_STEP = 8
NA_GROUP = 4
STREAM_QUERIES = 256
GQA_Q_TILE = STREAM_QUERIES
MLA_Q_TILE = 2 * STREAM_QUERIES
LOG2E = 1.4426950408889634
VMEM_LIMIT = 56 * 1024 * 1024
MASKED = -1e30

F32 = jnp.float32
BF16 = jnp.bfloat16
_NT = (((1,), (1,)), ((), ()))


def _params(n_axes):
    return pltpu.CompilerParams(dimension_semantics=("arbitrary",) * n_axes,
                                vmem_limit_bytes=VMEM_LIMIT)


def _rms(x, g):
    return x * lax.rsqrt(jnp.mean(x * x, axis=-1, keepdims=True) + EPS) * g


def _silu(g):
    return g * jax.nn.sigmoid(g)


def _proj0_kernel(x_ref, g_ref, w_ref, cos_ref, sin_ref, qg_ref, kg_ref,
                  qa_ref, ka_ref, va_ref, qb_ref, kb_ref, vt_ref, gate_ref):
    tm = x_ref.shape[1]
    h = _rms(x_ref[0], g_ref[...]).astype(BF16)

    a = jnp.dot(h, w_ref[:, 0:3 * A_WIDTH], preferred_element_type=F32)
    qa_ref[0] = (a[:, 0:A_WIDTH] * (HEAD_DIM ** -0.5 * LOG2E)).astype(BF16)
    ka_ref[0] = a[:, A_WIDTH:2 * A_WIDTH].astype(BF16)
    va_ref[0] = a[:, 2 * A_WIDTH:3 * A_WIDTH].astype(BF16)

    o = 3 * A_WIDTH
    b = jnp.dot(h, w_ref[:, o:o + B_WIDTH + 2 * B_KV_WIDTH], preferred_element_type=F32)
    cos = cos_ref[...]
    sin = sin_ref[...]
    lane = lax.broadcasted_iota(jnp.int32, (tm, LANES), 1)
    first = lane < HEAD_DIM
    low = (lane % HEAD_DIM) < HEAD_DIM // 2

    def norm_rope(xc, g):
        sq = xc * xc
        sa = jnp.sum(jnp.where(first, sq, 0.0), axis=-1, keepdims=True)
        sb = jnp.sum(jnp.where(first, 0.0, sq), axis=-1, keepdims=True)
        ms = jnp.where(first, sa, sb) * (1.0 / HEAD_DIM)
        y = xc * lax.rsqrt(ms + EPS) * g
        partner = jnp.where(low, pltpu.roll(y, LANES - HEAD_DIM // 2, 1),
                            pltpu.roll(y, HEAD_DIM // 2, 1))
        return y * cos + partner * sin

    qg = qg_ref[...]
    for j in range(B_WIDTH // LANES):
        qj = norm_rope(b[:, j * LANES:(j + 1) * LANES], qg)
        qb_ref[0, :, j * LANES:(j + 1) * LANES] = (qj * (HEAD_DIM ** -0.5 * LOG2E)).astype(BF16)
    kk = norm_rope(b[:, B_WIDTH:B_WIDTH + LANES], kg_ref[...])
    ks = pltpu.roll(kk, HEAD_DIM, 1)
    kb_ref[0, 0] = jnp.where(first, kk, ks).astype(BF16)
    kb_ref[0, 1] = jnp.where(first, ks, kk).astype(BF16)
    vt = b[:, B_WIDTH + LANES:B_WIDTH + 2 * LANES].T
    ones = jnp.ones((V_ROWS - HEAD_DIM, tm), BF16)
    for kv in range(B_KV_HEADS):
        vt_ref[0, kv, 0, 0:HEAD_DIM, :] = vt[kv * HEAD_DIM:(kv + 1) * HEAD_DIM].astype(BF16)
        vt_ref[0, kv, 0, HEAD_DIM:V_ROWS, :] = ones

    o += B_WIDTH + 2 * B_KV_WIDTH
    gate_ref[0] = jnp.dot(h, w_ref[:, o:o + D_MODEL], preferred_element_type=F32).astype(BF16)


def _proj0(x, g, w, cos, sin, qg, kg):
    bsz, s, _ = x.shape
    tm = TOKEN_TILE
    nck = s // tm
    tok = lambda width: pl.BlockSpec((1, tm, width), lambda b, i: (b, i, 0))
    const = lambda shape: pl.BlockSpec(shape, lambda b, i: (0,) * len(shape))
    return pl.pallas_call(
        _proj0_kernel,
        grid=(bsz, nck),
        in_specs=[tok(D_MODEL), const((1, D_MODEL)), const(w.shape),
                  pl.BlockSpec((tm, LANES), lambda b, i: (i, 0)),
                  pl.BlockSpec((tm, LANES), lambda b, i: (i, 0)),
                  const((1, LANES)), const((1, LANES))],
        out_specs=[tok(A_WIDTH), tok(A_WIDTH), tok(A_WIDTH), tok(B_WIDTH),
                   pl.BlockSpec((1, B_KV_HEADS, tm, LANES), lambda b, i: (b, 0, i, 0)),
                   pl.BlockSpec((1, B_KV_HEADS, 1, V_ROWS, tm), lambda b, i: (b, 0, i, 0, 0)),
                   tok(D_MODEL)],
        out_shape=[jax.ShapeDtypeStruct((bsz, s, A_WIDTH), BF16)] * 3
        + [jax.ShapeDtypeStruct((bsz, s, B_WIDTH), BF16),
           jax.ShapeDtypeStruct((bsz, B_KV_HEADS, s, LANES), BF16),
           jax.ShapeDtypeStruct((bsz, B_KV_HEADS, nck, V_ROWS, tm), BF16),
           jax.ShapeDtypeStruct((bsz, s, D_MODEL), BF16)],
        compiler_params=_params(2),
        name="proj0",
    )(x, g, w, cos, sin, qg, kg)


def _na_kernel(q_ref, k_ref, v_ref, bias_ref, o_ref, *, rows):
    t = pl.program_id(2)
    lane = lax.broadcasted_iota(jnp.int32, (GRID_W, LANES), 1)
    first = lane < HEAD_DIM
    win = NA_KH * GRID_W

    def body(g, carry):
        offs, scores = [], []
        for u in range(NA_GROUP):
            i = g * NA_GROUP + u
            r = t * NA_ROWS_PER_STEP + i
            rs = jnp.clip(r - NA_KH // 2, 0, rows - NA_KH)
            case = r - rs
            qoff = pl.multiple_of(i * GRID_W, GRID_W)
            koff = pl.multiple_of(rs * GRID_W, GRID_W)
            offs.append((qoff, koff))
            q = q_ref[0, pl.ds(qoff, GRID_W), :]
            kw = k_ref[0, pl.ds(koff, win), :]
            for hh in range(2):
                qm = jnp.where(first if hh == 0 else jnp.logical_not(first), q, jnp.zeros_like(q))
                scores.append(lax.dot_general(qm, kw, _NT, preferred_element_type=F32)
                              + bias_ref[hh, case])
        probs = []
        for sc in scores:
            e = jnp.exp2(sc - jnp.max(sc, axis=-1, keepdims=True))
            probs.append((e.astype(BF16), jnp.sum(e, axis=-1, keepdims=True)))
        for u, (qoff, koff) in enumerate(offs):
            vw = v_ref[0, pl.ds(koff, win), :]
            o0, o1 = [jnp.dot(e, vw, preferred_element_type=F32) / l for e, l in probs[2 * u:2 * u + 2]]
            o_ref[0, pl.ds(qoff, GRID_W), :] = jnp.where(first, o0, o1).astype(BF16)
        return carry

    lax.fori_loop(0, NA_ROWS_PER_STEP // NA_GROUP, body, 0)


def _na_bias_table(rpb):
    qc = np.arange(GRID_W)[:, None]
    kc = np.arange(GRID_W)[None, :]
    cs = np.clip(qc - NA_KW // 2, 0, GRID_W - NA_KW)
    valid = (kc >= cs) & (kc < cs + NA_KW)
    onehot = ((kc - qc + NA_KW - 1)[..., None] == np.arange(2 * NA_KW - 1)) & valid[..., None]
    cols = jnp.einsum("hrd,qkd->hrqk", rpb * LOG2E, jnp.asarray(onehot, F32),
                      precision=lax.Precision.HIGHEST)
    cols = jnp.where(valid[None, None], cols, MASKED)
    tbl = jnp.stack([cols[:, NA_KH - 1 - c:2 * NA_KH - 1 - c] for c in range(NA_KH)], axis=1)
    return tbl.transpose(0, 1, 3, 2, 4).reshape(rpb.shape[0], NA_KH, GRID_W, NA_KH * GRID_W)


def _na(qa, ka, va, bias):
    bsz, s, _ = qa.shape
    rows = s // GRID_W
    assert rows >= NA_KH and rows % NA_ROWS_PER_STEP == 0
    tq = NA_ROWS_PER_STEP * GRID_W
    return pl.pallas_call(
        functools.partial(_na_kernel, rows=rows),
        grid=(bsz, A_WIDTH // LANES, s // tq),
        in_specs=[pl.BlockSpec((1, tq, LANES), lambda b, hp, t: (b, t, hp)),
                  pl.BlockSpec((1, s, LANES), lambda b, hp, t: (b, 0, hp)),
                  pl.BlockSpec((1, s, LANES), lambda b, hp, t: (b, 0, hp)),
                  pl.BlockSpec((2, NA_KH, GRID_W, NA_KH * GRID_W), lambda b, hp, t: (hp, 0, 0, 0))],
        out_specs=pl.BlockSpec((1, tq, LANES), lambda b, hp, t: (b, t, hp)),
        out_shape=jax.ShapeDtypeStruct((bsz, s, A_WIDTH), BF16),
        compiler_params=_params(3),
        name="na",
    )(qa, ka, va, bias)


def _attend(streams, st_ref):
    nck, _, tk = streams[0][2].shape

    def scores(j, dst):
        off = pl.multiple_of(j * tk, tk)
        for i, (q, k_ref, _) in enumerate(streams):
            dst[i] = lax.dot_general(k_ref[pl.ds(off, tk), :], q, _NT,
                                     preferred_element_type=F32)

    def step(j, cur, nxt, carry):
        sm = []
        for i, (m, _) in enumerate(carry):
            st = cur[i]
            m_new = jnp.maximum(m, jnp.max(st, axis=0, keepdims=True))
            sm.append((m_new, jnp.exp2(m - m_new), jnp.exp2(st - m_new).astype(BF16)))
        if nxt is not None:
            scores(j + 1, nxt)
        pvs = [jnp.dot(vt_ref[j], p, preferred_element_type=F32)
               for (_, _, vt_ref), (_, _, p) in zip(streams, sm)]
        return tuple((m_new, acc * alpha + pv)
                     for (m_new, alpha, _), pv, (_, acc) in zip(sm, pvs, carry))

    def body(jj, carry):
        carry = step(2 * jj, st_ref[0], st_ref[1], carry)
        return step(2 * jj + 1, st_ref[1], st_ref[0], carry)

    scores(0, st_ref[0])
    carry = tuple((jnp.full((1, q.shape[0]), -jnp.inf, F32), jnp.zeros((V_ROWS, q.shape[0]), F32))
                  for q, _, _ in streams)
    pairs = (nck - 1) // 2
    carry = lax.fori_loop(0, pairs, body, carry)
    for j in range(2 * pairs, nck):
        carry = step(j, st_ref[j % 2], st_ref[(j + 1) % 2] if j + 1 < nck else None, carry)
    return [acc[0:HEAD_DIM] / acc[HEAD_DIM:HEAD_DIM + 1] for _, acc in carry]


def _score_scratch(n_streams, tk, n):
    return [pltpu.VMEM((n_streams, tk, n), F32)] * 2


def _gqa_kernel(q_ref, k_ref, vt_ref, o_ref, *st_ref):
    tq = q_ref.shape[1]
    q = q_ref[0]
    first = lax.broadcasted_iota(jnp.int32, (tq, LANES), 1) < HEAD_DIM
    zero = jnp.zeros((tq, LANES), BF16)
    heads = []
    for c in range(q.shape[1] // LANES):
        qc = q[:, c * LANES:(c + 1) * LANES]
        heads += [jnp.where(first, qc, zero), jnp.where(first, zero, qc)]
    outs = _attend([(qh, k_ref.at[0, 0], vt_ref.at[0, 0]) for qh in heads], st_ref)
    o_ref[0] = jnp.concatenate(outs, axis=0).T.astype(BF16)


def _gqa(qb, kb, vt):
    bsz, s, _ = qb.shape
    tq = GQA_Q_TILE
    gw = B_WIDTH // B_KV_HEADS
    return pl.pallas_call(
        _gqa_kernel,
        grid=(bsz, B_KV_HEADS, s // tq),
        in_specs=[pl.BlockSpec((1, tq, gw), lambda b, kv, i: (b, i, kv)),
                  pl.BlockSpec((1, 1, s, LANES), lambda b, kv, i: (b, kv, 0, 0)),
                  pl.BlockSpec((1, 1) + vt.shape[2:], lambda b, kv, i: (b, kv, 0, 0, 0))],
        out_specs=pl.BlockSpec((1, tq, gw), lambda b, kv, i: (b, i, kv)),
        out_shape=jax.ShapeDtypeStruct((bsz, s, B_WIDTH), BF16),
        scratch_shapes=_score_scratch(gw // HEAD_DIM, vt.shape[-1], tq),
        compiler_params=_params(3),
        name="gqa",
    )(qb, kb, vt)


def _mla_kernel(q_ref, k_ref, vt_ref, o_ref, *st_ref):
    tq = q_ref.shape[2]
    sub = st_ref[0].shape[-1]
    streams = [(q_ref[0, hh, c * sub:(c + 1) * sub, :], k_ref.at[0, hh], vt_ref.at[0, hh])
               for hh in range(2) for c in range(tq // sub)]
    outs = _attend(streams, st_ref)
    per_head = len(outs) // 2
    ot = jnp.concatenate([jnp.concatenate(outs[hh * per_head:(hh + 1) * per_head], axis=1)
                          for hh in range(2)], axis=0)
    o_ref[0] = ot.T.astype(BF16)


def _mla(q, k, vt):
    bsz, nh, s, _ = q.shape
    tq = MLA_Q_TILE
    return pl.pallas_call(
        _mla_kernel,
        grid=(bsz, nh // 2, s // tq),
        in_specs=[pl.BlockSpec((1, 2, tq, LANES), lambda b, hp, i: (b, hp, i, 0)),
                  pl.BlockSpec((1, 2, s, LANES), lambda b, hp, i: (b, hp, 0, 0)),
                  pl.BlockSpec((1, 2) + vt.shape[2:], lambda b, hp, i: (b, hp, 0, 0, 0))],
        out_specs=pl.BlockSpec((1, tq, LANES), lambda b, hp, i: (b, i, hp)),
        out_shape=jax.ShapeDtypeStruct((bsz, s, nh * C_V), BF16),
        scratch_shapes=_score_scratch(2 * tq // STREAM_QUERIES, vt.shape[-1], STREAM_QUERIES),
        compiler_params=_params(3),
        name="mla",
    )(q, k, vt)


def _out0_kernel(x_ref, ma_ref, mb_ref, gate_ref, w_ref, y_ref):
    sg = _silu(gate_ref[0].astype(F32))
    ga = (ma_ref[0].astype(F32) * sg[:, 0:A_WIDTH]).astype(BF16)
    gb = (mb_ref[0].astype(F32) * sg[:, A_WIDTH:]).astype(BF16)
    y_ref[0] = (x_ref[0] + jnp.dot(ga, w_ref[0:A_WIDTH, :], preferred_element_type=F32)
                + jnp.dot(gb, w_ref[A_WIDTH:, :], preferred_element_type=F32))


def _out0(x, ma, mb, gate, w):
    bsz, s, _ = x.shape
    tm = TOKEN_TILE
    tok = lambda width: pl.BlockSpec((1, tm, width), lambda b, i: (b, i, 0))
    return pl.pallas_call(
        _out0_kernel,
        grid=(bsz, s // tm),
        in_specs=[tok(D_MODEL), tok(A_WIDTH), tok(B_WIDTH), tok(D_MODEL),
                  pl.BlockSpec(w.shape, lambda b, i: (0, 0))],
        out_specs=tok(D_MODEL),
        out_shape=jax.ShapeDtypeStruct(x.shape, F32),
        compiler_params=_params(2),
        name="out0",
    )(x, ma, mb, gate, w)


def _out1_kernel(x_ref, m_ref, gate_ref, w_ref, g_ref, y_ref):
    gm = (m_ref[0].astype(F32) * _silu(gate_ref[0].astype(F32))).astype(BF16)
    y = x_ref[0] + jnp.dot(gm, w_ref[...], preferred_element_type=F32)
    y_ref[0] = _rms(y, g_ref[...])


def _out1(x, m, gate, w, g):
    bsz, s, _ = x.shape
    tm = TOKEN_TILE
    tok = pl.BlockSpec((1, tm, D_MODEL), lambda b, i: (b, i, 0))
    return pl.pallas_call(
        _out1_kernel,
        grid=(bsz, s // tm),
        in_specs=[tok, tok, tok, pl.BlockSpec(w.shape, lambda b, i: (0, 0)),
                  pl.BlockSpec((1, D_MODEL), lambda b, i: (0, 0))],
        out_specs=tok,
        out_shape=jax.ShapeDtypeStruct(x.shape, F32),
        compiler_params=_params(2),
        name="out1",
    )(x, m, gate, w, g)


def _proj1_kernel(x_ref, g_ref, w_ref, qg_ref, kvg_ref, wuq_ref, wuk_ref, wuv_ref, cos_ref, sin_ref,
                  q_ref, k_ref, vt_ref, gate_ref):
    tm = x_ref.shape[1]
    h = _rms(x_ref[0], g_ref[...]).astype(BF16)
    lat_w = C_Q_RANK + C_KV_RANK + LANES
    lat = jnp.dot(h, w_ref[:, 0:lat_w], preferred_element_type=F32)
    gate_ref[0] = jnp.dot(h, w_ref[:, lat_w:], preferred_element_type=F32).astype(BF16)

    cos = cos_ref[...]
    sin = sin_ref[...]
    lane = lax.broadcasted_iota(jnp.int32, (tm, LANES), 1)
    low = lane < C_NOPE + C_ROPE // 2

    def rope(xc):
        partner = jnp.where(low, pltpu.roll(xc, LANES - C_ROPE // 2, 1),
                            pltpu.roll(xc, C_ROPE // 2, 1))
        return xc * cos + partner * sin

    cq = _rms(lat[:, 0:C_Q_RANK], qg_ref[...]).astype(BF16)
    q_all = jnp.dot(cq, wuq_ref[...], preferred_element_type=F32)
    for hd in range(C_HEADS):
        qh = rope(q_all[:, hd * LANES:(hd + 1) * LANES])
        q_ref[0, hd] = (qh * (C_QK_DIM ** -0.5 * LOG2E)).astype(BF16)

    ckv = _rms(lat[:, C_Q_RANK:C_Q_RANK + C_KV_RANK], kvg_ref[...]).astype(BF16)
    k_rope = rope(lat[:, C_Q_RANK + C_KV_RANK:lat_w])
    k_all = jnp.dot(ckv, wuk_ref[...], preferred_element_type=F32)
    for hd in range(C_HEADS):
        k_ref[0, hd] = (k_all[:, hd * LANES:(hd + 1) * LANES] + k_rope).astype(BF16)

    vt = jnp.dot(ckv, wuv_ref[...], preferred_element_type=F32).T
    ones = jnp.ones((V_ROWS - C_V, tm), BF16)
    for hd in range(C_HEADS):
        vt_ref[0, hd, 0, 0:C_V, :] = vt[hd * C_V:(hd + 1) * C_V].astype(BF16)
        vt_ref[0, hd, 0, C_V:V_ROWS, :] = ones


def _proj1(x, g, w, qg, kvg, wuq, wuk, wuv, cos, sin):
    bsz, s, _ = x.shape
    tm = TOKEN_TILE
    nck = s // tm
    const = lambda shape: pl.BlockSpec(shape, lambda b, i: (0,) * len(shape))
    return pl.pallas_call(
        _proj1_kernel,
        grid=(bsz, nck),
        in_specs=[pl.BlockSpec((1, tm, D_MODEL), lambda b, i: (b, i, 0)),
                  const((1, D_MODEL)), const(w.shape), const((1, C_Q_RANK)), const((1, C_KV_RANK)),
                  const(wuq.shape), const(wuk.shape), const(wuv.shape),
                  pl.BlockSpec((tm, LANES), lambda b, i: (i, 0)),
                  pl.BlockSpec((tm, LANES), lambda b, i: (i, 0))],
        out_specs=[pl.BlockSpec((1, C_HEADS, tm, LANES), lambda b, i: (b, 0, i, 0)),
                   pl.BlockSpec((1, C_HEADS, tm, LANES), lambda b, i: (b, 0, i, 0)),
                   pl.BlockSpec((1, C_HEADS, 1, V_ROWS, tm), lambda b, i: (b, 0, i, 0, 0)),
                   pl.BlockSpec((1, tm, D_MODEL), lambda b, i: (b, i, 0))],
        out_shape=[jax.ShapeDtypeStruct((bsz, C_HEADS, s, LANES), BF16),
                   jax.ShapeDtypeStruct((bsz, C_HEADS, s, LANES), BF16),
                   jax.ShapeDtypeStruct((bsz, C_HEADS, nck, V_ROWS, tm), BF16),
                   jax.ShapeDtypeStruct((bsz, s, D_MODEL), BF16)],
        compiler_params=_params(2),
        name="proj1",
    )(x, g, w, qg, kvg, wuq, wuk, wuv, cos, sin)


def _axial_angles(n_tok, rot_dim):
    n_freq = rot_dim // 4
    inv = ROPE_THETA ** (-jnp.arange(n_freq, dtype=F32) / n_freq)
    t = jnp.arange(n_tok, dtype=jnp.int32)
    row = (t // GRID_W).astype(F32)
    col = (t % GRID_W).astype(F32)
    ang = jnp.concatenate([row[:, None] * inv[None], col[:, None] * inv[None]], axis=-1)
    return jnp.cos(ang), jnp.sin(ang)


def _rope_tables(s):
    c, sn = _axial_angles(s, HEAD_DIM)
    cos0 = jnp.tile(jnp.concatenate([c, c], axis=-1), (1, LANES // HEAD_DIM))
    sin0 = jnp.tile(jnp.concatenate([-sn, sn], axis=-1), (1, LANES // HEAD_DIM))
    c, sn = _axial_angles(s, C_ROPE)
    pad = LANES - C_QK_DIM
    cos1 = jnp.concatenate([jnp.ones((s, C_NOPE), F32), c, c, jnp.ones((s, pad), F32)], axis=-1)
    sin1 = jnp.concatenate([jnp.zeros((s, C_NOPE), F32), -sn, sn, jnp.zeros((s, pad), F32)], axis=-1)
    return cos0, sin0, cos1, sin1


def _prepare(norm_e, w_in_e, rpb_a, qnorm_b, knorm_b, w_out_e,
             norm_o, w_in_o, qlat_g, kvlat_g, w_uq, w_ukv, w_out_o, norm_f):
    rep = LANES // HEAD_DIM
    lat = C_Q_RANK + C_KV_RANK
    z = lambda n: jnp.zeros((D_MODEL, n), F32)
    w1 = jnp.concatenate([w_in_o[0][:, :lat], z(C_NOPE), w_in_o[0][:, lat:lat + C_ROPE],
                          z(LANES - C_QK_DIM), w_in_o[0][:, lat + C_ROPE:]], axis=1)
    wuq = jnp.pad(w_uq[0].reshape(C_Q_RANK, C_HEADS, C_QK_DIM),
                  ((0, 0), (0, 0), (0, LANES - C_QK_DIM))).reshape(C_Q_RANK, C_HEADS * LANES)
    wkv = w_ukv[0].reshape(C_KV_RANK, C_HEADS, C_NOPE + C_V)
    wuk = jnp.pad(wkv[:, :, :C_NOPE], ((0, 0), (0, 0), (0, LANES - C_NOPE))).reshape(C_KV_RANK, C_HEADS * LANES)
    wuv = wkv[:, :, C_NOPE:].reshape(C_KV_RANK, C_WIDTH)
    return dict(
        norm_e=norm_e[0][None], w_in_e=w_in_e[0].astype(BF16), bias=_na_bias_table(rpb_a[0]),
        qg=jnp.tile(qnorm_b[0], rep)[None], kg=jnp.tile(knorm_b[0], rep)[None],
        w_out_e=w_out_e[0].astype(BF16),
        norm_o=norm_o[0][None], w1=w1.astype(BF16), qlat_g=qlat_g[0][None], kvlat_g=kvlat_g[0][None],
        wuq=wuq.astype(BF16), wuk=wuk.astype(BF16), wuv=wuv.astype(BF16),
        w_out_o=w_out_o[0].astype(BF16), norm_f=norm_f[None])


def _trunk(x, p):
    cos0, sin0, cos1, sin1 = _rope_tables(x.shape[1])
    qa, ka, va, qb, kb, vbt, gate0 = _proj0(x, p["norm_e"], p["w_in_e"], cos0, sin0, p["qg"], p["kg"])
    mix_a = _na(qa, ka, va, p["bias"])
    mix_b = _gqa(qb, kb, vbt)
    x1 = _out0(x, mix_a, mix_b, gate0, p["w_out_e"])
    q, k, vt, gate1 = _proj1(x1, p["norm_o"], p["w1"], p["qlat_g"], p["kvlat_g"],
                             p["wuq"], p["wuk"], p["wuv"], cos1, sin1)
    mix_c = _mla(q, k, vt)
    return _out1(x1, mix_c, gate1, p["w_out_o"], p["norm_f"])


def kernel(x_prompt, x_sample, norm_e, w_in_e, rpb_a, qnorm_b, knorm_b, w_out_e,
           norm_o, w_in_o, qlat_g, kvlat_g, w_uq, w_ukv, w_out_o, norm_f):
    assert norm_e.shape[0] == 1 and norm_o.shape[0] == 1
    p = _prepare(norm_e, w_in_e, rpb_a, qnorm_b, knorm_b, w_out_e,
                 norm_o, w_in_o, qlat_g, kvlat_g, w_uq, w_ukv, w_out_o, norm_f)
    return (_trunk(x_prompt, p), _trunk(x_sample, p))
```

```python
import functools

import numpy as np
import jax
import jax.numpy as jnp
from jax import lax
from jax.experimental import pallas as pl
from jax.experimental.pallas import tpu as pltpu

D_MODEL = 1024
GRID_W = 64
HEAD_DIM = 64
A_HEADS = 8
NA_KH = 8
NA_KW = 16
B_HEADS = 8
B_KV_HEADS = 2
C_HEADS = 16
C_NOPE = 64
C_ROPE = 32
C_V = 64
C_Q_RANK = 384
C_KV_RANK = 256
ROPE_THETA = 10000.0
EPS = 1e-6

A_WIDTH = A_HEADS * HEAD_DIM
B_WIDTH = B_HEADS * HEAD_DIM
B_KV_WIDTH = B_KV_HEADS * HEAD_DIM
C_WIDTH = C_HEADS * C_V
C_QK_DIM = C_NOPE + C_ROPE

LANES = 128
MXU_TILE = 256
TOKEN_TILE = 512
V_ROWS = HEAD_DIM + 16
NA_ROWS_PER_STEP = 8
NA_GROUP = 4
STREAM_QUERIES = 256
GQA_Q_TILE = STREAM_QUERIES
MLA_Q_TILE = 2 * STREAM_QUERIES
STEPS_PER_TRIP = 10
LOG2E = 1.4426950408889634
VMEM_LIMIT = 56 * 1024 * 1024
MASKED = -1e30

F32 = jnp.float32
BF16 = jnp.bfloat16
_NT = (((1,), (1,)), ((), ()))


def _params(n_axes):
    return pltpu.CompilerParams(dimension_semantics=("arbitrary",) * n_axes,
                                vmem_limit_bytes=VMEM_LIMIT)


def _rms(x, g):
    return x * lax.rsqrt(jnp.mean(x * x, axis=-1, keepdims=True) + EPS) * g


def _silu(g):
    return g * jax.nn.sigmoid(g)


def _proj0_kernel(x_ref, g_ref, w_ref, cos_ref, sin_ref, qg_ref, kg_ref,
                  qa_ref, ka_ref, va_ref, qb_ref, kb_ref, vt_ref, gate_ref):
    tm = x_ref.shape[1]
    h = _rms(x_ref[0], g_ref[...]).astype(BF16)

    a = jnp.dot(h, w_ref[:, 0:3 * A_WIDTH], preferred_element_type=F32)
    qa_ref[0] = (a[:, 0:A_WIDTH] * (HEAD_DIM ** -0.5 * LOG2E)).astype(BF16)
    ka_ref[0] = a[:, A_WIDTH:2 * A_WIDTH].astype(BF16)
    va_ref[0] = a[:, 2 * A_WIDTH:3 * A_WIDTH].astype(BF16)

    o = 3 * A_WIDTH
    b = jnp.dot(h, w_ref[:, o:o + B_WIDTH + 2 * B_KV_WIDTH], preferred_element_type=F32)
    cos = cos_ref[...]
    sin = sin_ref[...]
    lane = lax.broadcasted_iota(jnp.int32, (tm, LANES), 1)
    first = lane < HEAD_DIM
    low = (lane % HEAD_DIM) < HEAD_DIM // 2

    def norm_rope(xc, g):
        sq = xc * xc
        sa = jnp.sum(jnp.where(first, sq, 0.0), axis=-1, keepdims=True)
        sb = jnp.sum(jnp.where(first, 0.0, sq), axis=-1, keepdims=True)
        ms = jnp.where(first, sa, sb) * (1.0 / HEAD_DIM)
        y = xc * lax.rsqrt(ms + EPS) * g
        partner = jnp.where(low, pltpu.roll(y, LANES - HEAD_DIM // 2, 1),
                            pltpu.roll(y, HEAD_DIM // 2, 1))
        return y * cos + partner * sin

    qg = qg_ref[...]
    for j in range(B_WIDTH // LANES):
        qj = norm_rope(b[:, j * LANES:(j + 1) * LANES], qg)
        qt = (qj * (HEAD_DIM ** -0.5 * LOG2E)).T.astype(BF16)
        upper = lax.broadcasted_iota(jnp.int32, qt.shape, 0) < HEAD_DIM
        qb_ref[0, 2 * j] = jnp.where(upper, qt, jnp.zeros_like(qt))
        qb_ref[0, 2 * j + 1] = jnp.where(upper, jnp.zeros_like(qt), qt)
    kk = norm_rope(b[:, B_WIDTH:B_WIDTH + LANES], kg_ref[...])
    ks = pltpu.roll(kk, HEAD_DIM, 1)
    kb_ref[0, 0] = jnp.where(first, kk, ks).astype(BF16)
    kb_ref[0, 1] = jnp.where(first, ks, kk).astype(BF16)
    vt = b[:, B_WIDTH + LANES:B_WIDTH + 2 * LANES].T
    ones = jnp.ones((V_ROWS - HEAD_DIM, tm), BF16)
    for kv in range(B_KV_HEADS):
        vt_ref[0, kv, 0, 0:HEAD_DIM, :] = vt[kv * HEAD_DIM:(kv + 1) * HEAD_DIM].astype(BF16)
        vt_ref[0, kv, 0, HEAD_DIM:V_ROWS, :] = ones

    o += B_WIDTH + 2 * B_KV_WIDTH
    gate_ref[0] = jnp.dot(h, w_ref[:, o:o + D_MODEL], preferred_element_type=F32).astype(BF16)


def _proj0(x, g, w, cos, sin, qg, kg):
    bsz, s, _ = x.shape
    tm = TOKEN_TILE
    nck = s // tm
    tok = lambda width: pl.BlockSpec((1, tm, width), lambda b, i: (b, i, 0))
    const = lambda shape: pl.BlockSpec(shape, lambda b, i: (0,) * len(shape))
    return pl.pallas_call(
        _proj0_kernel,
        grid=(bsz, nck),
        in_specs=[tok(D_MODEL), const((1, D_MODEL)), const(w.shape),
                  pl.BlockSpec((tm, LANES), lambda b, i: (i, 0)),
                  pl.BlockSpec((tm, LANES), lambda b, i: (i, 0)),
                  const((1, LANES)), const((1, LANES))],
        out_specs=[tok(A_WIDTH), tok(A_WIDTH), tok(A_WIDTH),
                   pl.BlockSpec((1, B_HEADS, LANES, tm), lambda b, i: (b, 0, 0, i)),
                   pl.BlockSpec((1, B_KV_HEADS, tm, LANES), lambda b, i: (b, 0, i, 0)),
                   pl.BlockSpec((1, B_KV_HEADS, 1, V_ROWS, tm), lambda b, i: (b, 0, i, 0, 0)),
                   tok(D_MODEL)],
        out_shape=[jax.ShapeDtypeStruct((bsz, s, A_WIDTH), BF16)] * 3
        + [jax.ShapeDtypeStruct((bsz, B_HEADS, LANES, s), BF16),
           jax.ShapeDtypeStruct((bsz, B_KV_HEADS, s, LANES), BF16),
           jax.ShapeDtypeStruct((bsz, B_KV_HEADS, nck, V_ROWS, tm), BF16),
           jax.ShapeDtypeStruct((bsz, s, D_MODEL), BF16)],
        compiler_params=_params(2),
        name="proj0",
    )(x, g, w, cos, sin, qg, kg)


def _na_kernel(q_ref, k_ref, v_ref, bias_ref, o_ref, *, rows):
    t = pl.program_id(2)
    lane = lax.broadcasted_iota(jnp.int32, (GRID_W, LANES), 1)
    first = lane < HEAD_DIM
    win = NA_KH * GRID_W

    def body(g, carry):
        offs, scores = [], []
        for u in range(NA_GROUP):
            i = g * NA_GROUP + u
            r = t * NA_ROWS_PER_STEP + i
            rs = jnp.clip(r - NA_KH // 2, 0, rows - NA_KH)
            case = r - rs
            qoff = pl.multiple_of(i * GRID_W, GRID_W)
            koff = pl.multiple_of(rs * GRID_W, GRID_W)
            offs.append((qoff, koff))
            q = q_ref[0, pl.ds(qoff, GRID_W), :]
            kw = k_ref[0, pl.ds(koff, win), :]
            for hh in range(2):
                qm = jnp.where(first if hh == 0 else jnp.logical_not(first), q, jnp.zeros_like(q))
                scores.append(lax.dot_general(qm, kw, _NT, preferred_element_type=F32)
                              + bias_ref[hh, case])
        probs = []
        for sc in scores:
            e = jnp.exp2(sc - jnp.max(sc, axis=-1, keepdims=True))
            probs.append((e.astype(BF16), jnp.sum(e, axis=-1, keepdims=True)))
        for u, (qoff, koff) in enumerate(offs):
            vw = v_ref[0, pl.ds(koff, win), :]
            o0, o1 = [jnp.dot(e, vw, preferred_element_type=F32) / l for e, l in probs[2 * u:2 * u + 2]]
            o_ref[0, pl.ds(qoff, GRID_W), :] = jnp.where(first, o0, o1).astype(BF16)
        return carry

    lax.fori_loop(0, NA_ROWS_PER_STEP // NA_GROUP, body, 0)


def _na_bias_table(rpb):
    qc = np.arange(GRID_W)[:, None]
    kc = np.arange(GRID_W)[None, :]
    cs = np.clip(qc - NA_KW // 2, 0, GRID_W - NA_KW)
    valid = (kc >= cs) & (kc < cs + NA_KW)
    onehot = ((kc - qc + NA_KW - 1)[..., None] == np.arange(2 * NA_KW - 1)) & valid[..., None]
    cols = jnp.einsum("hrd,qkd->hrqk", rpb * LOG2E, jnp.asarray(onehot, F32),
                      precision=lax.Precision.HIGHEST)
    cols = jnp.where(valid[None, None], cols, MASKED)
    tbl = jnp.stack([cols[:, NA_KH - 1 - c:2 * NA_KH - 1 - c] for c in range(NA_KH)], axis=1)
    return tbl.transpose(0, 1, 3, 2, 4).reshape(rpb.shape[0], NA_KH, GRID_W, NA_KH * GRID_W)


def _na(qa, ka, va, bias):
    bsz, s, _ = qa.shape
    rows = s // GRID_W
    assert rows >= NA_KH and rows % NA_ROWS_PER_STEP == 0
    tq = NA_ROWS_PER_STEP * GRID_W
    return pl.pallas_call(
        functools.partial(_na_kernel, rows=rows),
        grid=(bsz, A_WIDTH // LANES, s // tq),
        in_specs=[pl.BlockSpec((1, tq, LANES), lambda b, hp, t: (b, t, hp)),
                  pl.BlockSpec((1, s, LANES), lambda b, hp, t: (b, 0, hp)),
                  pl.BlockSpec((1, s, LANES), lambda b, hp, t: (b, 0, hp)),
                  pl.BlockSpec((2, NA_KH, GRID_W, NA_KH * GRID_W), lambda b, hp, t: (hp, 0, 0, 0))],
        out_specs=pl.BlockSpec((1, tq, LANES), lambda b, hp, t: (b, t, hp)),
        out_shape=jax.ShapeDtypeStruct((bsz, s, A_WIDTH), BF16),
        compiler_params=_params(3),
        name="na",
    )(qa, ka, va, bias)


def _attend(streams, st_ref, p_ref):
    nck, _, tk = streams[0][2].shape
    n = streams[0][0].shape[1]

    def scores(i, j, par, h):
        q, k_ref, _ = streams[i]
        off = pl.multiple_of(j * tk + h * MXU_TILE, MXU_TILE)
        st = jnp.dot(k_ref[pl.ds(off, MXU_TILE), :], q, preferred_element_type=F32)
        st_ref[par][i, h * MXU_TILE:(h + 1) * MXU_TILE, :] = st
        return jnp.max(st, axis=0, keepdims=True)

    def softmax(i, par, m, cmax):
        m_new = jnp.maximum(m, cmax)
        p_ref[par][i] = jnp.exp2(st_ref[par][i] - m_new).astype(BF16)
        return m_new, jnp.exp2(m - m_new)

    def step(j, par, carry, with_scores=True, with_softmax=True):
        out = []
        for i, (m, alpha, acc, cmax) in enumerate(carry):
            acc = acc * alpha
            cmax_next = cmax
            for h in range(tk // MXU_TILE):
                if with_scores:
                    c = scores(i, j + 2, par, h)
                    cmax_next = c if h == 0 else jnp.maximum(cmax_next, c)
                rows = slice(h * MXU_TILE, (h + 1) * MXU_TILE)
                acc = acc + jnp.dot(streams[i][2][j, :, rows], p_ref[par][i, rows, :],
                                    preferred_element_type=F32)
            if with_softmax:
                m, alpha = softmax(i, 1 - par, m, cmax)
            out.append((m, alpha, acc, cmax_next))
        return tuple(out)

    def body(jj, carry):
        for u in range(STEPS_PER_TRIP):
            carry = step(STEPS_PER_TRIP * jj + u, u % 2, carry)
        return carry

    def chunk_scores(i, j, par):
        return functools.reduce(jnp.maximum, [scores(i, j, par, h) for h in range(tk // MXU_TILE)])

    carry = []
    for i in range(len(streams)):
        cmax0 = chunk_scores(i, 0, 0)
        cmax1 = chunk_scores(i, 1, 1) if nck > 1 else cmax0
        m, alpha = softmax(i, 0, jnp.full((1, n), -jnp.inf, F32), cmax0)
        carry.append((m, alpha, jnp.zeros((V_ROWS, n), F32), cmax1))
    carry = tuple(carry)
    trips = max(nck - 2, 0) // STEPS_PER_TRIP
    carry = lax.fori_loop(0, trips, body, carry)
    for j in range(STEPS_PER_TRIP * trips, nck):
        carry = step(j, j % 2, carry, with_scores=j + 2 < nck, with_softmax=j + 1 < nck)
    return [acc[0:HEAD_DIM] / acc[HEAD_DIM:HEAD_DIM + 1] for _, _, acc, _ in carry]


def _attend_scratch(n_streams, tk, n):
    return ([pltpu.VMEM((n_streams, tk, n), F32)] * 2 + [pltpu.VMEM((n_streams, tk, n), BF16)] * 2)


def _gqa_kernel(q_ref, k_ref, vt_ref, o_ref, st0, st1, p0, p1):
    streams = [(q_ref[0, h], k_ref.at[0, 0], vt_ref.at[0, 0]) for h in range(q_ref.shape[1])]
    outs = _attend(streams, (st0, st1), (p0, p1))
    o_ref[0] = jnp.concatenate(outs, axis=0).T.astype(BF16)


def _gqa(qb, kb, vt):
    bsz, _, _, s = qb.shape
    tq = GQA_Q_TILE
    group = B_HEADS // B_KV_HEADS
    gw = group * HEAD_DIM
    return pl.pallas_call(
        _gqa_kernel,
        grid=(bsz, B_KV_HEADS, s // tq),
        in_specs=[pl.BlockSpec((1, group, LANES, tq), lambda b, kv, i: (b, kv, 0, i)),
                  pl.BlockSpec((1, 1, s, LANES), lambda b, kv, i: (b, kv, 0, 0)),
                  pl.BlockSpec((1, 1) + vt.shape[2:], lambda b, kv, i: (b, kv, 0, 0, 0))],
        out_specs=pl.BlockSpec((1, tq, gw), lambda b, kv, i: (b, i, kv)),
        out_shape=jax.ShapeDtypeStruct((bsz, s, B_WIDTH), BF16),
        scratch_shapes=_attend_scratch(group, vt.shape[-1], tq),
        compiler_params=_params(3),
        name="gqa",
    )(qb, kb, vt)


def _mla_kernel(q_ref, k_ref, vt_ref, o_ref, st0, st1, p0, p1):
    tq = q_ref.shape[3]
    sub = st0.shape[-1]
    streams = [(q_ref[0, hh, :, c * sub:(c + 1) * sub], k_ref.at[0, hh], vt_ref.at[0, hh])
               for hh in range(2) for c in range(tq // sub)]
    outs = _attend(streams, (st0, st1), (p0, p1))
    per_head = len(outs) // 2
    ot = jnp.concatenate([jnp.concatenate(outs[hh * per_head:(hh + 1) * per_head], axis=1)
                          for hh in range(2)], axis=0)
    o_ref[0] = ot.T.astype(BF16)


def _mla(q, k, vt):
    bsz, nh, _, s = q.shape
    tq = MLA_Q_TILE
    return pl.pallas_call(
        _mla_kernel,
        grid=(bsz, nh // 2, s // tq),
        in_specs=[pl.BlockSpec((1, 2, LANES, tq), lambda b, hp, i: (b, hp, 0, i)),
                  pl.BlockSpec((1, 2, s, LANES), lambda b, hp, i: (b, hp, 0, 0)),
                  pl.BlockSpec((1, 2) + vt.shape[2:], lambda b, hp, i: (b, hp, 0, 0, 0))],
        out_specs=pl.BlockSpec((1, tq, LANES), lambda b, hp, i: (b, i, hp)),
        out_shape=jax.ShapeDtypeStruct((bsz, s, nh * C_V), BF16),
        scratch_shapes=_attend_scratch(2 * tq // STREAM_QUERIES, vt.shape[-1], STREAM_QUERIES),
        compiler_params=_params(3),
        name="mla",
    )(q, k, vt)


def _out0_kernel(x_ref, ma_ref, mb_ref, gate_ref, w_ref, y_ref):
    sg = _silu(gate_ref[0].astype(F32))
    ga = (ma_ref[0].astype(F32) * sg[:, 0:A_WIDTH]).astype(BF16)
    gb = (mb_ref[0].astype(F32) * sg[:, A_WIDTH:]).astype(BF16)
    y_ref[0] = (x_ref[0] + jnp.dot(ga, w_ref[0:A_WIDTH, :], preferred_element_type=F32)
                + jnp.dot(gb, w_ref[A_WIDTH:, :], preferred_element_type=F32))


def _out0(x, ma, mb, gate, w):
    bsz, s, _ = x.shape
    tm = TOKEN_TILE
    tok = lambda width: pl.BlockSpec((1, tm, width), lambda b, i: (b, i, 0))
    return pl.pallas_call(
        _out0_kernel,
        grid=(bsz, s // tm),
        in_specs=[tok(D_MODEL), tok(A_WIDTH), tok(B_WIDTH), tok(D_MODEL),
                  pl.BlockSpec(w.shape, lambda b, i: (0, 0))],
        out_specs=tok(D_MODEL),
        out_shape=jax.ShapeDtypeStruct(x.shape, F32),
        compiler_params=_params(2),
        name="out0",
    )(x, ma, mb, gate, w)


def _out1_kernel(x_ref, m_ref, gate_ref, w_ref, g_ref, y_ref):
    gm = (m_ref[0].astype(F32) * _silu(gate_ref[0].astype(F32))).astype(BF16)
    y = x_ref[0] + jnp.dot(gm, w_ref[...], preferred_element_type=F32)
    y_ref[0] = _rms(y, g_ref[...])


def _out1(x, m, gate, w, g):
    bsz, s, _ = x.shape
    tm = TOKEN_TILE
    tok = pl.BlockSpec((1, tm, D_MODEL), lambda b, i: (b, i, 0))
    return pl.pallas_call(
        _out1_kernel,
        grid=(bsz, s // tm),
        in_specs=[tok, tok, tok, pl.BlockSpec(w.shape, lambda b, i: (0, 0)),
                  pl.BlockSpec((1, D_MODEL), lambda b, i: (0, 0))],
        out_specs=tok,
        out_shape=jax.ShapeDtypeStruct(x.shape, F32),
        compiler_params=_params(2),
        name="out1",
    )(x, m, gate, w, g)


def _proj1_kernel(x_ref, g_ref, w_ref, qg_ref, kvg_ref, wuq_ref, wuk_ref, wuv_ref, cos_ref, sin_ref,
                  q_ref, k_ref, vt_ref, gate_ref):
    tm = x_ref.shape[1]
    h = _rms(x_ref[0], g_ref[...]).astype(BF16)
    lat_w = C_Q_RANK + C_KV_RANK + LANES
    lat = jnp.dot(h, w_ref[:, 0:lat_w], preferred_element_type=F32)
    gate_ref[0] = jnp.dot(h, w_ref[:, lat_w:], preferred_element_type=F32).astype(BF16)

    cos = cos_ref[...]
    sin = sin_ref[...]
    lane = lax.broadcasted_iota(jnp.int32, (tm, LANES), 1)
    low = lane < C_NOPE + C_ROPE // 2

    def rope(xc):
        partner = jnp.where(low, pltpu.roll(xc, LANES - C_ROPE // 2, 1),
                            pltpu.roll(xc, C_ROPE // 2, 1))
        return xc * cos + partner * sin

    cq = _rms(lat[:, 0:C_Q_RANK], qg_ref[...]).astype(BF16)
    q_all = jnp.dot(cq, wuq_ref[...], preferred_element_type=F32)
    for hd in range(C_HEADS):
        qh = rope(q_all[:, hd * LANES:(hd + 1) * LANES])
        q_ref[0, hd] = (qh * (C_QK_DIM ** -0.5 * LOG2E)).T.astype(BF16)

    ckv = _rms(lat[:, C_Q_RANK:C_Q_RANK + C_KV_RANK], kvg_ref[...]).astype(BF16)
    k_rope = rope(lat[:, C_Q_RANK + C_KV_RANK:lat_w])
    k_all = jnp.dot(ckv, wuk_ref[...], preferred_element_type=F32)
    for hd in range(C_HEADS):
        k_ref[0, hd] = (k_all[:, hd * LANES:(hd + 1) * LANES] + k_rope).astype(BF16)

    vt = jnp.dot(ckv, wuv_ref[...], preferred_element_type=F32).T
    ones = jnp.ones((V_ROWS - C_V, tm), BF16)
    for hd in range(C_HEADS):
        vt_ref[0, hd, 0, 0:C_V, :] = vt[hd * C_V:(hd + 1) * C_V].astype(BF16)
        vt_ref[0, hd, 0, C_V:V_ROWS, :] = ones


def _proj1(x, g, w, qg, kvg, wuq, wuk, wuv, cos, sin):
    bsz, s, _ = x.shape
    tm = TOKEN_TILE
    nck = s // tm
    const = lambda shape: pl.BlockSpec(shape, lambda b, i: (0,) * len(shape))
    return pl.pallas_call(
        _proj1_kernel,
        grid=(bsz, nck),
        in_specs=[pl.BlockSpec((1, tm, D_MODEL), lambda b, i: (b, i, 0)),
                  const((1, D_MODEL)), const(w.shape), const((1, C_Q_RANK)), const((1, C_KV_RANK)),
                  const(wuq.shape), const(wuk.shape), const(wuv.shape),
                  pl.BlockSpec((tm, LANES), lambda b, i: (i, 0)),
                  pl.BlockSpec((tm, LANES), lambda b, i: (i, 0))],
        out_specs=[pl.BlockSpec((1, C_HEADS, LANES, tm), lambda b, i: (b, 0, 0, i)),
                   pl.BlockSpec((1, C_HEADS, tm, LANES), lambda b, i: (b, 0, i, 0)),
                   pl.BlockSpec((1, C_HEADS, 1, V_ROWS, tm), lambda b, i: (b, 0, i, 0, 0)),
                   pl.BlockSpec((1, tm, D_MODEL), lambda b, i: (b, i, 0))],
        out_shape=[jax.ShapeDtypeStruct((bsz, C_HEADS, LANES, s), BF16),
                   jax.ShapeDtypeStruct((bsz, C_HEADS, s, LANES), BF16),
                   jax.ShapeDtypeStruct((bsz, C_HEADS, nck, V_ROWS, tm), BF16),
                   jax.ShapeDtypeStruct((bsz, s, D_MODEL), BF16)],
        compiler_params=_params(2),
        name="proj1",
    )(x, g, w, qg, kvg, wuq, wuk, wuv, cos, sin)


def _axial_angles(n_tok, rot_dim):
    n_freq = rot_dim // 4
    inv = ROPE_THETA ** (-jnp.arange(n_freq, dtype=F32) / n_freq)
    t = jnp.arange(n_tok, dtype=jnp.int32)
    row = (t // GRID_W).astype(F32)
    col = (t % GRID_W).astype(F32)
    ang = jnp.concatenate([row[:, None] * inv[None], col[:, None] * inv[None]], axis=-1)
    return jnp.cos(ang), jnp.sin(ang)


def _rope_tables(s):
    c, sn = _axial_angles(s, HEAD_DIM)
    cos0 = jnp.tile(jnp.concatenate([c, c], axis=-1), (1, LANES // HEAD_DIM))
    sin0 = jnp.tile(jnp.concatenate([-sn, sn], axis=-1), (1, LANES // HEAD_DIM))
    c, sn = _axial_angles(s, C_ROPE)
    pad = LANES - C_QK_DIM
    cos1 = jnp.concatenate([jnp.ones((s, C_NOPE), F32), c, c, jnp.ones((s, pad), F32)], axis=-1)
    sin1 = jnp.concatenate([jnp.zeros((s, C_NOPE), F32), -sn, sn, jnp.zeros((s, pad), F32)], axis=-1)
    return cos0, sin0, cos1, sin1


def _prepare(norm_e, w_in_e, rpb_a, qnorm_b, knorm_b, w_out_e,
             norm_o, w_in_o, qlat_g, kvlat_g, w_uq, w_ukv, w_out_o, norm_f):
    rep = LANES // HEAD_DIM
    lat = C_Q_RANK + C_KV_RANK
    z = lambda n: jnp.zeros((D_MODEL, n), F32)
    w1 = jnp.concatenate([w_in_o[0][:, :lat], z(C_NOPE), w_in_o[0][:, lat:lat + C_ROPE],
                          z(LANES - C_QK_DIM), w_in_o[0][:, lat + C_ROPE:]], axis=1)
    wuq = jnp.pad(w_uq[0].reshape(C_Q_RANK, C_HEADS, C_QK_DIM),
                  ((0, 0), (0, 0), (0, LANES - C_QK_DIM))).reshape(C_Q_RANK, C_HEADS * LANES)
    wkv = w_ukv[0].reshape(C_KV_RANK, C_HEADS, C_NOPE + C_V)
    wuk = jnp.pad(wkv[:, :, :C_NOPE], ((0, 0), (0, 0), (0, LANES - C_NOPE))).reshape(C_KV_RANK, C_HEADS * LANES)
    wuv = wkv[:, :, C_NOPE:].reshape(C_KV_RANK, C_WIDTH)
    return dict(
        norm_e=norm_e[0][None], w_in_e=w_in_e[0].astype(BF16), bias=_na_bias_table(rpb_a[0]),
        qg=jnp.tile(qnorm_b[0], rep)[None], kg=jnp.tile(knorm_b[0], rep)[None],
        w_out_e=w_out_e[0].astype(BF16),
        norm_o=norm_o[0][None], w1=w1.astype(BF16), qlat_g=qlat_g[0][None], kvlat_g=kvlat_g[0][None],
        wuq=wuq.astype(BF16), wuk=wuk.astype(BF16), wuv=wuv.astype(BF16),
        w_out_o=w_out_o[0].astype(BF16), norm_f=norm_f[None])


def _trunk(x, p):
    cos0, sin0, cos1, sin1 = _rope_tables(x.shape[1])
    qa, ka, va, qb, kb, vbt, gate0 = _proj0(x, p["norm_e"], p["w_in_e"], cos0, sin0, p["qg"], p["kg"])
    mix_a = _na(qa, ka, va, p["bias"])
    mix_b = _gqa(qb, kb, vbt)
    x1 = _out0(x, mix_a, mix_b, gate0, p["w_out_e"])
    q, k, vt, gate1 = _proj1(x1, p["norm_o"], p["w1"], p["qlat_g"], p["kvlat_g"],
                             p["wuq"], p["wuk"], p["wuv"], cos1, sin1)
    mix_c = _mla(q, k, vt)
    return _out1(x1, mix_c, gate1, p["w_out_o"], p["norm_f"])


def kernel(x_prompt, x_sample, norm_e, w_in_e, rpb_a, qnorm_b, knorm_b, w_out_e,
           norm_o, w_in_o, qlat_g, kvlat_g, w_uq, w_ukv, w_out_o, norm_f):
    assert norm_e.shape[0] == 1 and norm_o.shape[0] == 1
    p = _prepare(norm_e, w_in_e, rpb_a, qnorm_b, knorm_b, w_out_e,
                 norm_o, w_in_o, qlat_g, kvlat_g, w_uq, w_ukv, w_out_o, norm_f)
    return (_trunk(x_prompt, p), _trunk(x_sample, p))
```

```python
import functools

import numpy as np
import jax
import jax.numpy as jnp
from jax import lax
from jax.experimental import pallas as pl
from jax.experimental.pallas import tpu as pltpu

D_MODEL = 1024
GRID_W = 64
HEAD_DIM = 64
A_HEADS = 8
NA_KH = 8
NA_KW = 16
B_HEADS = 8
B_KV_HEADS = 2
C_HEADS = 16
C_NOPE = 64
C_ROPE = 32
C_V = 64
C_Q_RANK = 384
C_KV_RANK = 256
ROPE_THETA = 10000.0
EPS = 1e-6

A_WIDTH = A_HEADS * HEAD_DIM
B_WIDTH = B_HEADS * HEAD_DIM
B_KV_WIDTH = B_KV_HEADS * HEAD_DIM
C_WIDTH = C_HEADS * C_V
C_QK_DIM = C_NOPE + C_ROPE

LANES = 128
SUBLANES = 8
MXU_TILE = 256
SOFTMAX_SLAB = 32
TOKEN_TILE = 512
V_ROWS = HEAD_DIM + 16
NA_ROWS_PER_STEP = 8
NA_GROUP = 4
STREAM_QUERIES = 256
GQA_Q_TILE = STREAM_QUERIES
MLA_Q_TILE = 2 * STREAM_QUERIES
RING = 3
STEPS_PER_TRIP = 12
LOG2E = 1.4426950408889634
VMEM_LIMIT = 56 * 1024 * 1024
MASKED = -1e30

F32 = jnp.float32
BF16 = jnp.bfloat16
_NT = (((1,), (1,)), ((), ()))


def _params(n_axes):
    return pltpu.CompilerParams(dimension_semantics=("arbitrary",) * n_axes,
                                vmem_limit_bytes=VMEM_LIMIT)


def _rms(x, g):
    return x * lax.rsqrt(jnp.mean(x * x, axis=-1, keepdims=True) + EPS) * g


def _silu(g):
    return g * jax.nn.sigmoid(g)


def _proj0_kernel(x_ref, g_ref, w_ref, cos_ref, sin_ref, qg_ref, kg_ref,
                  qa_ref, ka_ref, va_ref, qb_ref, kb_ref, vt_ref, gate_ref):
    tm = x_ref.shape[1]
    h = _rms(x_ref[0], g_ref[...]).astype(BF16)

    a = jnp.dot(h, w_ref[:, 0:3 * A_WIDTH], preferred_element_type=F32)
    qa_ref[0] = (a[:, 0:A_WIDTH] * (HEAD_DIM ** -0.5 * LOG2E)).astype(BF16)
    ka_ref[0] = a[:, A_WIDTH:2 * A_WIDTH].astype(BF16)
    va_ref[0] = a[:, 2 * A_WIDTH:3 * A_WIDTH].astype(BF16)

    o = 3 * A_WIDTH
    b = jnp.dot(h, w_ref[:, o:o + B_WIDTH + 2 * B_KV_WIDTH], preferred_element_type=F32)
    cos = cos_ref[...]
    sin = sin_ref[...]
    lane = lax.broadcasted_iota(jnp.int32, (tm, LANES), 1)
    first = lane < HEAD_DIM
    low = (lane % HEAD_DIM) < HEAD_DIM // 2

    def norm_rope(xc, g):
        sq = xc * xc
        sa = jnp.sum(jnp.where(first, sq, 0.0), axis=-1, keepdims=True)
        sb = jnp.sum(jnp.where(first, 0.0, sq), axis=-1, keepdims=True)
        ms = jnp.where(first, sa, sb) * (1.0 / HEAD_DIM)
        y = xc * lax.rsqrt(ms + EPS) * g
        partner = jnp.where(low, pltpu.roll(y, LANES - HEAD_DIM // 2, 1),
                            pltpu.roll(y, HEAD_DIM // 2, 1))
        return y * cos + partner * sin

    qg = qg_ref[...]
    for j in range(B_WIDTH // LANES):
        qj = norm_rope(b[:, j * LANES:(j + 1) * LANES], qg)
        qt = (qj * (HEAD_DIM ** -0.5 * LOG2E)).T.astype(BF16)
        upper = lax.broadcasted_iota(jnp.int32, qt.shape, 0) < HEAD_DIM
        qb_ref[0, 2 * j] = jnp.where(upper, qt, jnp.zeros_like(qt))
        qb_ref[0, 2 * j + 1] = jnp.where(upper, jnp.zeros_like(qt), qt)
    kk = norm_rope(b[:, B_WIDTH:B_WIDTH + LANES], kg_ref[...])
    ks = pltpu.roll(kk, HEAD_DIM, 1)
    kb_ref[0, 0] = jnp.where(first, kk, ks).astype(BF16)
    kb_ref[0, 1] = jnp.where(first, ks, kk).astype(BF16)
    vt = b[:, B_WIDTH + LANES:B_WIDTH + 2 * LANES].T
    ones = jnp.ones((V_ROWS - HEAD_DIM, tm), BF16)
    for kv in range(B_KV_HEADS):
        vt_ref[0, kv, 0, 0:HEAD_DIM, :] = vt[kv * HEAD_DIM:(kv + 1) * HEAD_DIM].astype(BF16)
        vt_ref[0, kv, 0, HEAD_DIM:V_ROWS, :] = ones

    o += B_WIDTH + 2 * B_KV_WIDTH
    gate_ref[0] = jnp.dot(h, w_ref[:, o:o + D_MODEL], preferred_element_type=F32).astype(BF16)


def _proj0(x, g, w, cos, sin, qg, kg):
    bsz, s, _ = x.shape
    tm = TOKEN_TILE
    nck = s // tm
    tok = lambda width: pl.BlockSpec((1, tm, width), lambda b, i: (b, i, 0))
    const = lambda shape: pl.BlockSpec(shape, lambda b, i: (0,) * len(shape))
    return pl.pallas_call(
        _proj0_kernel,
        grid=(bsz, nck),
        in_specs=[tok(D_MODEL), const((1, D_MODEL)), const(w.shape),
                  pl.BlockSpec((tm, LANES), lambda b, i: (i, 0)),
                  pl.BlockSpec((tm, LANES), lambda b, i: (i, 0)),
                  const((1, LANES)), const((1, LANES))],
        out_specs=[tok(A_WIDTH), tok(A_WIDTH), tok(A_WIDTH),
                   pl.BlockSpec((1, B_HEADS, LANES, tm), lambda b, i: (b, 0, 0, i)),
                   pl.BlockSpec((1, B_KV_HEADS, tm, LANES), lambda b, i: (b, 0, i, 0)),
                   pl.BlockSpec((1, B_KV_HEADS, 1, V_ROWS, tm), lambda b, i: (b, 0, i, 0, 0)),
                   tok(D_MODEL)],
        out_shape=[jax.ShapeDtypeStruct((bsz, s, A_WIDTH), BF16)] * 3
        + [jax.ShapeDtypeStruct((bsz, B_HEADS, LANES, s), BF16),
           jax.ShapeDtypeStruct((bsz, B_KV_HEADS, s, LANES), BF16),
           jax.ShapeDtypeStruct((bsz, B_KV_HEADS, nck, V_ROWS, tm), BF16),
           jax.ShapeDtypeStruct((bsz, s, D_MODEL), BF16)],
        compiler_params=_params(2),
        name="proj0",
    )(x, g, w, cos, sin, qg, kg)


def _na_kernel(q_ref, k_ref, v_ref, bias_ref, o_ref, *, rows):
    t = pl.program_id(2)
    lane = lax.broadcasted_iota(jnp.int32, (GRID_W, LANES), 1)
    first = lane < HEAD_DIM
    win = NA_KH * GRID_W

    def body(g, carry):
        offs, scores = [], []
        for u in range(NA_GROUP):
            i = g * NA_GROUP + u
            r = t * NA_ROWS_PER_STEP + i
            rs = jnp.clip(r - NA_KH // 2, 0, rows - NA_KH)
            case = r - rs
            qoff = pl.multiple_of(i * GRID_W, GRID_W)
            koff = pl.multiple_of(rs * GRID_W, GRID_W)
            offs.append((qoff, koff))
            q = q_ref[0, pl.ds(qoff, GRID_W), :]
            kw = k_ref[0, pl.ds(koff, win), :]
            for hh in range(2):
                qm = jnp.where(first if hh == 0 else jnp.logical_not(first), q, jnp.zeros_like(q))
                scores.append(lax.dot_general(qm, kw, _NT, preferred_element_type=F32)
                              + bias_ref[hh, case])
        probs = []
        for sc in scores:
            e = jnp.exp2(sc - jnp.max(sc, axis=-1, keepdims=True))
            probs.append((e.astype(BF16), jnp.sum(e, axis=-1, keepdims=True)))
        for u, (qoff, koff) in enumerate(offs):
            vw = v_ref[0, pl.ds(koff, win), :]
            o0, o1 = [jnp.dot(e, vw, preferred_element_type=F32) / l for e, l in probs[2 * u:2 * u + 2]]
            o_ref[0, pl.ds(qoff, GRID_W), :] = jnp.where(first, o0, o1).astype(BF16)
        return carry

    lax.fori_loop(0, NA_ROWS_PER_STEP // NA_GROUP, body, 0)


def _na_bias_table(rpb):
    qc = np.arange(GRID_W)[:, None]
    kc = np.arange(GRID_W)[None, :]
    cs = np.clip(qc - NA_KW // 2, 0, GRID_W - NA_KW)
    valid = (kc >= cs) & (kc < cs + NA_KW)
    onehot = ((kc - qc + NA_KW - 1)[..., None] == np.arange(2 * NA_KW - 1)) & valid[..., None]
    cols = jnp.einsum("hrd,qkd->hrqk", rpb * LOG2E, jnp.asarray(onehot, F32),
                      precision=lax.Precision.HIGHEST)
    cols = jnp.where(valid[None, None], cols, MASKED)
    tbl = jnp.stack([cols[:, NA_KH - 1 - c:2 * NA_KH - 1 - c] for c in range(NA_KH)], axis=1)
    return tbl.transpose(0, 1, 3, 2, 4).reshape(rpb.shape[0], NA_KH, GRID_W, NA_KH * GRID_W)


def _na(qa, ka, va, bias):
    bsz, s, _ = qa.shape
    rows = s // GRID_W
    assert rows >= NA_KH and rows % NA_ROWS_PER_STEP == 0
    tq = NA_ROWS_PER_STEP * GRID_W
    return pl.pallas_call(
        functools.partial(_na_kernel, rows=rows),
        grid=(bsz, A_WIDTH // LANES, s // tq),
        in_specs=[pl.BlockSpec((1, tq, LANES), lambda b, hp, t: (b, t, hp)),
                  pl.BlockSpec((1, s, LANES), lambda b, hp, t: (b, 0, hp)),
                  pl.BlockSpec((1, s, LANES), lambda b, hp, t: (b, 0, hp)),
                  pl.BlockSpec((2, NA_KH, GRID_W, NA_KH * GRID_W), lambda b, hp, t: (hp, 0, 0, 0))],
        out_specs=pl.BlockSpec((1, tq, LANES), lambda b, hp, t: (b, t, hp)),
        out_shape=jax.ShapeDtypeStruct((bsz, s, A_WIDTH), BF16),
        compiler_params=_params(3),
        name="na",
    )(qa, ka, va, bias)


def _attend(streams, st_ref, p_ref):
    nck, _, tk = streams[0][2].shape
    n = streams[0][0].shape[1]

    def scores(i, j, par, h):
        q, k_ref, _ = streams[i]
        off = pl.multiple_of(j * tk + h * MXU_TILE, MXU_TILE)
        st = jnp.dot(k_ref[pl.ds(off, MXU_TILE), :], q, preferred_element_type=F32)
        st_ref[par][i, h * MXU_TILE:(h + 1) * MXU_TILE, :] = st
        c = st[0:SUBLANES]
        for r in range(SUBLANES, MXU_TILE, SUBLANES):
            c = jnp.maximum(c, st[r:r + SUBLANES])
        return c

    def softmax(i, par, m, cmax):
        m_new = jnp.maximum(m, jnp.max(cmax, axis=0, keepdims=True))
        for r in range(0, tk, SOFTMAX_SLAB):
            rows = slice(r, r + SOFTMAX_SLAB)
            p_ref[par][i, rows, :] = jnp.exp2(st_ref[par][i, rows, :] - m_new).astype(BF16)
        return m_new, jnp.exp2(m - m_new)

    def step(j, par, carry, with_scores=True, with_softmax=True):
        out = []
        for i, (m, alpha, acc, cmax) in enumerate(carry):
            acc = acc * alpha
            cmax_next = cmax
            for h in range(tk // MXU_TILE):
                if with_scores:
                    c = scores(i, j + 2, (par + 2) % RING, h)
                    cmax_next = c if h == 0 else jnp.maximum(cmax_next, c)
                rows = slice(h * MXU_TILE, (h + 1) * MXU_TILE)
                acc = acc + jnp.dot(streams[i][2][j, :, rows], p_ref[par][i, rows, :],
                                    preferred_element_type=F32)
            if with_softmax:
                m, alpha = softmax(i, (par + 1) % RING, m, cmax)
            out.append((m, alpha, acc, cmax_next))
        return tuple(out)

    def body(jj, carry):
        for u in range(STEPS_PER_TRIP):
            carry = step(STEPS_PER_TRIP * jj + u, u % RING, carry)
        return carry

    def chunk_scores(i, j, par):
        return functools.reduce(jnp.maximum, [scores(i, j, par, h) for h in range(tk // MXU_TILE)])

    carry = []
    for i in range(len(streams)):
        cmax0 = chunk_scores(i, 0, 0)
        cmax1 = chunk_scores(i, 1, 1) if nck > 1 else cmax0
        m, alpha = softmax(i, 0, jnp.full((1, n), -jnp.inf, F32), cmax0)
        carry.append((m, alpha, jnp.zeros((V_ROWS, n), F32), cmax1))
    carry = tuple(carry)
    trips = max(nck - 2, 0) // STEPS_PER_TRIP
    carry = lax.fori_loop(0, trips, body, carry)
    for j in range(STEPS_PER_TRIP * trips, nck):
        carry = step(j, j % RING, carry, with_scores=j + 2 < nck, with_softmax=j + 1 < nck)
    return [acc[0:HEAD_DIM] / acc[HEAD_DIM:HEAD_DIM + 1] for _, _, acc, _ in carry]


def _attend_scratch(n_streams, tk, n):
    return ([pltpu.VMEM((n_streams, tk, n), F32)] * RING + [pltpu.VMEM((n_streams, tk, n), BF16)] * RING)


def _gqa_kernel(q_ref, k_ref, vt_ref, o_ref, *scratch):
    streams = [(q_ref[0, h], k_ref.at[0, 0], vt_ref.at[0, 0]) for h in range(q_ref.shape[1])]
    outs = _attend(streams, scratch[:RING], scratch[RING:])
    o_ref[0] = jnp.concatenate(outs, axis=0).T.astype(BF16)


def _gqa(qb, kb, vt):
    bsz, _, _, s = qb.shape
    tq = GQA_Q_TILE
    group = B_HEADS // B_KV_HEADS
    gw = group * HEAD_DIM
    return pl.pallas_call(
        _gqa_kernel,
        grid=(bsz, B_KV_HEADS, s // tq),
        in_specs=[pl.BlockSpec((1, group, LANES, tq), lambda b, kv, i: (b, kv, 0, i)),
                  pl.BlockSpec((1, 1, s, LANES), lambda b, kv, i: (b, kv, 0, 0)),
                  pl.BlockSpec((1, 1) + vt.shape[2:], lambda b, kv, i: (b, kv, 0, 0, 0))],
        out_specs=pl.BlockSpec((1, tq, gw), lambda b, kv, i: (b, i, kv)),
        out_shape=jax.ShapeDtypeStruct((bsz, s, B_WIDTH), BF16),
        scratch_shapes=_attend_scratch(group, vt.shape[-1], tq),
        compiler_params=_params(3),
        name="gqa",
    )(qb, kb, vt)


def _mla_kernel(q_ref, k_ref, vt_ref, o_ref, *scratch):
    tq = q_ref.shape[3]
    sub = scratch[0].shape[-1]
    streams = [(q_ref[0, hh, :, c * sub:(c + 1) * sub], k_ref.at[0, hh], vt_ref.at[0, hh])
               for hh in range(2) for c in range(tq // sub)]
    outs = _attend(streams, scratch[:RING], scratch[RING:])
    per_head = len(outs) // 2
    ot = jnp.concatenate([jnp.concatenate(outs[hh * per_head:(hh + 1) * per_head], axis=1)
                          for hh in range(2)], axis=0)
    o_ref[0] = ot.T.astype(BF16)


def _mla(q, k, vt):
    bsz, nh, _, s = q.shape
    tq = MLA_Q_TILE
    return pl.pallas_call(
        _mla_kernel,
        grid=(bsz, nh // 2, s // tq),
        in_specs=[pl.BlockSpec((1, 2, LANES, tq), lambda b, hp, i: (b, hp, 0, i)),
                  pl.BlockSpec((1, 2, s, LANES), lambda b, hp, i: (b, hp, 0, 0)),
                  pl.BlockSpec((1, 2) + vt.shape[2:], lambda b, hp, i: (b, hp, 0, 0, 0))],
        out_specs=pl.BlockSpec((1, tq, LANES), lambda b, hp, i: (b, i, hp)),
        out_shape=jax.ShapeDtypeStruct((bsz, s, nh * C_V), BF16),
        scratch_shapes=_attend_scratch(2 * tq // STREAM_QUERIES, vt.shape[-1], STREAM_QUERIES),
        compiler_params=_params(3),
        name="mla",
    )(q, k, vt)


def _out0_kernel(x_ref, ma_ref, mb_ref, gate_ref, w_ref, y_ref):
    sg = _silu(gate_ref[0].astype(F32))
    ga = (ma_ref[0].astype(F32) * sg[:, 0:A_WIDTH]).astype(BF16)
    gb = (mb_ref[0].astype(F32) * sg[:, A_WIDTH:]).astype(BF16)
    y_ref[0] = (x_ref[0] + jnp.dot(ga, w_ref[0:A_WIDTH, :], preferred_element_type=F32)
                + jnp.dot(gb, w_ref[A_WIDTH:, :], preferred_element_type=F32))


def _out0(x, ma, mb, gate, w):
    bsz, s, _ = x.shape
    tm = TOKEN_TILE
    tok = lambda width: pl.BlockSpec((1, tm, width), lambda b, i: (b, i, 0))
    return pl.pallas_call(
        _out0_kernel,
        grid=(bsz, s // tm),
        in_specs=[tok(D_MODEL), tok(A_WIDTH), tok(B_WIDTH), tok(D_MODEL),
                  pl.BlockSpec(w.shape, lambda b, i: (0, 0))],
        out_specs=tok(D_MODEL),
        out_shape=jax.ShapeDtypeStruct(x.shape, F32),
        compiler_params=_params(2),
        name="out0",
    )(x, ma, mb, gate, w)


def _out1_kernel(x_ref, m_ref, gate_ref, w_ref, g_ref, y_ref):
    gm = (m_ref[0].astype(F32) * _silu(gate_ref[0].astype(F32))).astype(BF16)
    y = x_ref[0] + jnp.dot(gm, w_ref[...], preferred_element_type=F32)
    y_ref[0] = _rms(y, g_ref[...])


def _out1(x, m, gate, w, g):
    bsz, s, _ = x.shape
    tm = TOKEN_TILE
    tok = pl.BlockSpec((1, tm, D_MODEL), lambda b, i: (b, i, 0))
    return pl.pallas_call(
        _out1_kernel,
        grid=(bsz, s // tm),
        in_specs=[tok, tok, tok, pl.BlockSpec(w.shape, lambda b, i: (0, 0)),
                  pl.BlockSpec((1, D_MODEL), lambda b, i: (0, 0))],
        out_specs=tok,
        out_shape=jax.ShapeDtypeStruct(x.shape, F32),
        compiler_params=_params(2),
        name="out1",
    )(x, m, gate, w, g)


def _proj1_kernel(x_ref, g_ref, w_ref, qg_ref, kvg_ref, wuq_ref, wuk_ref, wuv_ref, cos_ref, sin_ref,
                  q_ref, k_ref, vt_ref, gate_ref):
    tm = x_ref.shape[1]
    h = _rms(x_ref[0], g_ref[...]).astype(BF16)
    lat_w = C_Q_RANK + C_KV_RANK + LANES
    lat = jnp.dot(h, w_ref[:, 0:lat_w], preferred_element_type=F32)
    gate_ref[0] = jnp.dot(h, w_ref[:, lat_w:], preferred_element_type=F32).astype(BF16)

    cos = cos_ref[...]
    sin = sin_ref[...]
    lane = lax.broadcasted_iota(jnp.int32, (tm, LANES), 1)
    low = lane < C_NOPE + C_ROPE // 2

    def rope(xc):
        partner = jnp.where(low, pltpu.roll(xc, LANES - C_ROPE // 2, 1),
                            pltpu.roll(xc, C_ROPE // 2, 1))
        return xc * cos + partner * sin

    cq = _rms(lat[:, 0:C_Q_RANK], qg_ref[...]).astype(BF16)
    q_all = jnp.dot(cq, wuq_ref[...], preferred_element_type=F32)
    for hd in range(C_HEADS):
        qh = rope(q_all[:, hd * LANES:(hd + 1) * LANES])
        q_ref[0, hd] = (qh * (C_QK_DIM ** -0.5 * LOG2E)).T.astype(BF16)

    ckv = _rms(lat[:, C_Q_RANK:C_Q_RANK + C_KV_RANK], kvg_ref[...]).astype(BF16)
    k_rope = rope(lat[:, C_Q_RANK + C_KV_RANK:lat_w])
    k_all = jnp.dot(ckv, wuk_ref[...], preferred_element_type=F32)
    for hd in range(C_HEADS):
        k_ref[0, hd] = (k_all[:, hd * LANES:(hd + 1) * LANES] + k_rope).astype(BF16)

    vt = jnp.dot(ckv, wuv_ref[...], preferred_element_type=F32).T
    ones = jnp.ones((V_ROWS - C_V, tm), BF16)
    for hd in range(C_HEADS):
        vt_ref[0, hd, 0, 0:C_V, :] = vt[hd * C_V:(hd + 1) * C_V].astype(BF16)
        vt_ref[0, hd, 0, C_V:V_ROWS, :] = ones


def _proj1(x, g, w, qg, kvg, wuq, wuk, wuv, cos, sin):
    bsz, s, _ = x.shape
    tm = TOKEN_TILE
    nck = s // tm
    const = lambda shape: pl.BlockSpec(shape, lambda b, i: (0,) * len(shape))
    return pl.pallas_call(
        _proj1_kernel,
        grid=(bsz, nck),
        in_specs=[pl.BlockSpec((1, tm, D_MODEL), lambda b, i: (b, i, 0)),
                  const((1, D_MODEL)), const(w.shape), const((1, C_Q_RANK)), const((1, C_KV_RANK)),
                  const(wuq.shape), const(wuk.shape), const(wuv.shape),
                  pl.BlockSpec((tm, LANES), lambda b, i: (i, 0)),
                  pl.BlockSpec((tm, LANES), lambda b, i: (i, 0))],
        out_specs=[pl.BlockSpec((1, C_HEADS, LANES, tm), lambda b, i: (b, 0, 0, i)),
                   pl.BlockSpec((1, C_HEADS, tm, LANES), lambda b, i: (b, 0, i, 0)),
                   pl.BlockSpec((1, C_HEADS, 1, V_ROWS, tm), lambda b, i: (b, 0, i, 0, 0)),
                   pl.BlockSpec((1, tm, D_MODEL), lambda b, i: (b, i, 0))],
        out_shape=[jax.ShapeDtypeStruct((bsz, C_HEADS, LANES, s), BF16),
                   jax.ShapeDtypeStruct((bsz, C_HEADS, s, LANES), BF16),
                   jax.ShapeDtypeStruct((bsz, C_HEADS, nck, V_ROWS, tm), BF16),
                   jax.ShapeDtypeStruct((bsz, s, D_MODEL), BF16)],
        compiler_params=_params(2),
        name="proj1",
    )(x, g, w, qg, kvg, wuq, wuk, wuv, cos, sin)


def _axial_angles(n_tok, rot_dim):
    n_freq = rot_dim // 4
    inv = ROPE_THETA ** (-jnp.arange(n_freq, dtype=F32) / n_freq)
    t = jnp.arange(n_tok, dtype=jnp.int32)
    row = (t // GRID_W).astype(F32)
    col = (t % GRID_W).astype(F32)
    ang = jnp.concatenate([row[:, None] * inv[None], col[:, None] * inv[None]], axis=-1)
    return jnp.cos(ang), jnp.sin(ang)


def _rope_tables(s):
    c, sn = _axial_angles(s, HEAD_DIM)
    cos0 = jnp.tile(jnp.concatenate([c, c], axis=-1), (1, LANES // HEAD_DIM))
    sin0 = jnp.tile(jnp.concatenate([-sn, sn], axis=-1), (1, LANES // HEAD_DIM))
    c, sn = _axial_angles(s, C_ROPE)
    pad = LANES - C_QK_DIM
    cos1 = jnp.concatenate([jnp.ones((s, C_NOPE), F32), c, c, jnp.ones((s, pad), F32)], axis=-1)
    sin1 = jnp.concatenate([jnp.zeros((s, C_NOPE), F32), -sn, sn, jnp.zeros((s, pad), F32)], axis=-1)
    return cos0, sin0, cos1, sin1


def _prepare(norm_e, w_in_e, rpb_a, qnorm_b, knorm_b, w_out_e,
             norm_o, w_in_o, qlat_g, kvlat_g, w_uq, w_ukv, w_out_o, norm_f):
    rep = LANES // HEAD_DIM
    lat = C_Q_RANK + C_KV_RANK
    z = lambda n: jnp.zeros((D_MODEL, n), F32)
    w1 = jnp.concatenate([w_in_o[0][:, :lat], z(C_NOPE), w_in_o[0][:, lat:lat + C_ROPE],
                          z(LANES - C_QK_DIM), w_in_o[0][:, lat + C_ROPE:]], axis=1)
    wuq = jnp.pad(w_uq[0].reshape(C_Q_RANK, C_HEADS, C_QK_DIM),
                  ((0, 0), (0, 0), (0, LANES - C_QK_DIM))).reshape(C_Q_RANK, C_HEADS * LANES)
    wkv = w_ukv[0].reshape(C_KV_RANK, C_HEADS, C_NOPE + C_V)
    wuk = jnp.pad(wkv[:, :, :C_NOPE], ((0, 0), (0, 0), (0, LANES - C_NOPE))).reshape(C_KV_RANK, C_HEADS * LANES)
    wuv = wkv[:, :, C_NOPE:].reshape(C_KV_RANK, C_WIDTH)
    return dict(
        norm_e=norm_e[0][None], w_in_e=w_in_e[0].astype(BF16), bias=_na_bias_table(rpb_a[0]),
        qg=jnp.tile(qnorm_b[0], rep)[None], kg=jnp.tile(knorm_b[0], rep)[None],
        w_out_e=w_out_e[0].astype(BF16),
        norm_o=norm_o[0][None], w1=w1.astype(BF16), qlat_g=qlat_g[0][None], kvlat_g=kvlat_g[0][None],
        wuq=wuq.astype(BF16), wuk=wuk.astype(BF16), wuv=wuv.astype(BF16),
        w_out_o=w_out_o[0].astype(BF16), norm_f=norm_f[None])


def _trunk(x, p):
    cos0, sin0, cos1, sin1 = _rope_tables(x.shape[1])
    qa, ka, va, qb, kb, vbt, gate0 = _proj0(x, p["norm_e"], p["w_in_e"], cos0, sin0, p["qg"], p["kg"])
    mix_a = _na(qa, ka, va, p["bias"])
    mix_b = _gqa(qb, kb, vbt)
    x1 = _out0(x, mix_a, mix_b, gate0, p["w_out_e"])
    q, k, vt, gate1 = _proj1(x1, p["norm_o"], p["w1"], p["qlat_g"], p["kvlat_g"],
                             p["wuq"], p["wuk"], p["wuv"], cos1, sin1)
    mix_c = _mla(q, k, vt)
    return _out1(x1, mix_c, gate1, p["w_out_o"], p["norm_f"])


def kernel(x_prompt, x_sample, norm_e, w_in_e, rpb_a, qnorm_b, knorm_b, w_out_e,
           norm_o, w_in_o, qlat_g, kvlat_g, w_uq, w_ukv, w_out_o, norm_f):
    assert norm_e.shape[0] == 1 and norm_o.shape[0] == 1
    p = _prepare(norm_e, w_in_e, rpb_a, qnorm_b, knorm_b, w_out_e,
                 norm_o, w_in_o, qlat_g, kvlat_g, w_uq, w_ukv, w_out_o, norm_f)
    return (_trunk(x_prompt, p), _trunk(x_sample, p))
```

```python
import functools

import numpy as np
import jax
import jax.numpy as jnp
from jax import lax
from jax.experimental import pallas as pl
from jax.experimental.pallas import tpu as pltpu

D_MODEL = 1024
GRID_W = 64
HEAD_DIM = 64
A_HEADS = 8
NA_KH = 8
NA_KW = 16
B_HEADS = 8
B_KV_HEADS = 2
C_HEADS = 16
C_NOPE = 64
C_ROPE = 32
C_V = 64
C_Q_RANK = 384
C_KV_RANK = 256
ROPE_THETA = 10000.0
EPS = 1e-6

A_WIDTH = A_HEADS * HEAD_DIM
B_WIDTH = B_HEADS * HEAD_DIM
B_KV_WIDTH = B_KV_HEADS * HEAD_DIM
C_WIDTH = C_HEADS * C_V
C_QK_DIM = C_NOPE + C_ROPE

LANES = 128
SUBLANES = 8
MXU_TILE = 256
SOFTMAX_SLAB = 32
TOKEN_TILE = 512
V_ROWS = HEAD_DIM + 16
NA_ROWS_PER_STEP = 8
NA_GROUP = 8
STREAM_QUERIES = 256
GQA_Q_TILE = STREAM_QUERIES
MLA_Q_TILE = 2 * STREAM_QUERIES
RING = 3
STEPS_PER_TRIP = 30
LOG2E = 1.4426950408889634
VMEM_LIMIT = 56 * 1024 * 1024
MASKED = -1e30

F32 = jnp.float32
BF16 = jnp.bfloat16
_NT = (((1,), (1,)), ((), ()))


def _params(n_axes):
    return pltpu.CompilerParams(dimension_semantics=("arbitrary",) * n_axes,
                                vmem_limit_bytes=VMEM_LIMIT)


def _rms(x, g):
    return x * lax.rsqrt(jnp.mean(x * x, axis=-1, keepdims=True) + EPS) * g


def _silu(g):
    return g * jax.nn.sigmoid(g)


def _proj0_kernel(x_ref, g_ref, w_ref, cos_ref, sin_ref, qg_ref, kg_ref,
                  qa_ref, ka_ref, va_ref, qb_ref, kb_ref, vt_ref, gate_ref):
    tm = x_ref.shape[1]
    h = _rms(x_ref[0], g_ref[...]).astype(BF16)

    a = jnp.dot(h, w_ref[:, 0:3 * A_WIDTH], preferred_element_type=F32)
    qa_ref[0] = (a[:, 0:A_WIDTH] * (HEAD_DIM ** -0.5 * LOG2E)).astype(BF16)
    ka_ref[0] = a[:, A_WIDTH:2 * A_WIDTH].astype(BF16)
    va_ref[0] = a[:, 2 * A_WIDTH:3 * A_WIDTH].astype(BF16)

    o = 3 * A_WIDTH
    b = jnp.dot(h, w_ref[:, o:o + B_WIDTH + 2 * B_KV_WIDTH], preferred_element_type=F32)
    cos = cos_ref[...]
    sin = sin_ref[...]
    lane = lax.broadcasted_iota(jnp.int32, (tm, LANES), 1)
    first = lane < HEAD_DIM
    low = (lane % HEAD_DIM) < HEAD_DIM // 2

    def norm_rope(xc, g):
        sq = xc * xc
        sa = jnp.sum(jnp.where(first, sq, 0.0), axis=-1, keepdims=True)
        sb = jnp.sum(jnp.where(first, 0.0, sq), axis=-1, keepdims=True)
        ms = jnp.where(first, sa, sb) * (1.0 / HEAD_DIM)
        y = xc * lax.rsqrt(ms + EPS) * g
        partner = jnp.where(low, pltpu.roll(y, LANES - HEAD_DIM // 2, 1),
                            pltpu.roll(y, HEAD_DIM // 2, 1))
        return y * cos + partner * sin

    qg = qg_ref[...]
    for j in range(B_WIDTH // LANES):
        qj = norm_rope(b[:, j * LANES:(j + 1) * LANES], qg)
        qt = (qj * (HEAD_DIM ** -0.5 * LOG2E)).T.astype(BF16)
        upper = lax.broadcasted_iota(jnp.int32, qt.shape, 0) < HEAD_DIM
        qb_ref[0, 2 * j] = jnp.where(upper, qt, jnp.zeros_like(qt))
        qb_ref[0, 2 * j + 1] = jnp.where(upper, jnp.zeros_like(qt), qt)
    kk = norm_rope(b[:, B_WIDTH:B_WIDTH + LANES], kg_ref[...])
    ks = pltpu.roll(kk, HEAD_DIM, 1)
    kb_ref[0, 0] = jnp.where(first, kk, ks).astype(BF16)
    kb_ref[0, 1] = jnp.where(first, ks, kk).astype(BF16)
    vt = b[:, B_WIDTH + LANES:B_WIDTH + 2 * LANES].T
    ones = jnp.ones((V_ROWS - HEAD_DIM, tm), BF16)
    for kv in range(B_KV_HEADS):
        vt_ref[0, kv, 0, 0:HEAD_DIM, :] = vt[kv * HEAD_DIM:(kv + 1) * HEAD_DIM].astype(BF16)
        vt_ref[0, kv, 0, HEAD_DIM:V_ROWS, :] = ones

    o += B_WIDTH + 2 * B_KV_WIDTH
    gate_ref[0] = jnp.dot(h, w_ref[:, o:o + D_MODEL], preferred_element_type=F32).astype(BF16)


def _proj0(x, g, w, cos, sin, qg, kg):
    bsz, s, _ = x.shape
    tm = TOKEN_TILE
    nck = s // tm
    tok = lambda width: pl.BlockSpec((1, tm, width), lambda b, i: (b, i, 0))
    const = lambda shape: pl.BlockSpec(shape, lambda b, i: (0,) * len(shape))
    return pl.pallas_call(
        _proj0_kernel,
        grid=(bsz, nck),
        in_specs=[tok(D_MODEL), const((1, D_MODEL)), const(w.shape),
                  pl.BlockSpec((tm, LANES), lambda b, i: (i, 0)),
                  pl.BlockSpec((tm, LANES), lambda b, i: (i, 0)),
                  const((1, LANES)), const((1, LANES))],
        out_specs=[tok(A_WIDTH), tok(A_WIDTH), tok(A_WIDTH),
                   pl.BlockSpec((1, B_HEADS, LANES, tm), lambda b, i: (b, 0, 0, i)),
                   pl.BlockSpec((1, B_KV_HEADS, tm, LANES), lambda b, i: (b, 0, i, 0)),
                   pl.BlockSpec((1, B_KV_HEADS, 1, V_ROWS, tm), lambda b, i: (b, 0, i, 0, 0)),
                   tok(D_MODEL)],
        out_shape=[jax.ShapeDtypeStruct((bsz, s, A_WIDTH), BF16)] * 3
        + [jax.ShapeDtypeStruct((bsz, B_HEADS, LANES, s), BF16),
           jax.ShapeDtypeStruct((bsz, B_KV_HEADS, s, LANES), BF16),
           jax.ShapeDtypeStruct((bsz, B_KV_HEADS, nck, V_ROWS, tm), BF16),
           jax.ShapeDtypeStruct((bsz, s, D_MODEL), BF16)],
        compiler_params=_params(2),
        name="proj0",
    )(x, g, w, cos, sin, qg, kg)


def _na_kernel(q_ref, k_ref, v_ref, bias_ref, o_ref, *, rows):
    t = pl.program_id(2)
    lane = lax.broadcasted_iota(jnp.int32, (GRID_W, LANES), 1)
    first = lane < HEAD_DIM
    win = NA_KH * GRID_W

    def body(g, carry):
        offs, scores = [], []
        for u in range(NA_GROUP):
            i = g * NA_GROUP + u
            r = t * NA_ROWS_PER_STEP + i
            rs = jnp.clip(r - NA_KH // 2, 0, rows - NA_KH)
            case = r - rs
            qoff = pl.multiple_of(i * GRID_W, GRID_W)
            koff = pl.multiple_of(rs * GRID_W, GRID_W)
            offs.append((qoff, koff))
            q = q_ref[0, pl.ds(qoff, GRID_W), :]
            kw = k_ref[0, pl.ds(koff, win), :]
            for hh in range(2):
                qm = jnp.where(first if hh == 0 else jnp.logical_not(first), q, jnp.zeros_like(q))
                scores.append(lax.dot_general(qm, kw, _NT, preferred_element_type=F32)
                              + bias_ref[hh, case])
        probs = []
        for sc in scores:
            e = jnp.exp2(sc - jnp.max(sc, axis=-1, keepdims=True))
            probs.append((e.astype(BF16), jnp.sum(e, axis=-1, keepdims=True)))
        for u, (qoff, koff) in enumerate(offs):
            vw = v_ref[0, pl.ds(koff, win), :]
            o0, o1 = [jnp.dot(e, vw, preferred_element_type=F32) / l for e, l in probs[2 * u:2 * u + 2]]
            o_ref[0, pl.ds(qoff, GRID_W), :] = jnp.where(first, o0, o1).astype(BF16)
        return carry

    lax.fori_loop(0, NA_ROWS_PER_STEP // NA_GROUP, body, 0)


def _na_bias_table(rpb):
    qc = np.arange(GRID_W)[:, None]
    kc = np.arange(GRID_W)[None, :]
    cs = np.clip(qc - NA_KW // 2, 0, GRID_W - NA_KW)
    valid = (kc >= cs) & (kc < cs + NA_KW)
    onehot = ((kc - qc + NA_KW - 1)[..., None] == np.arange(2 * NA_KW - 1)) & valid[..., None]
    cols = jnp.einsum("hrd,qkd->hrqk", rpb * LOG2E, jnp.asarray(onehot, F32),
                      precision=lax.Precision.HIGHEST)
    cols = jnp.where(valid[None, None], cols, MASKED)
    tbl = jnp.stack([cols[:, NA_KH - 1 - c:2 * NA_KH - 1 - c] for c in range(NA_KH)], axis=1)
    return tbl.transpose(0, 1, 3, 2, 4).reshape(rpb.shape[0], NA_KH, GRID_W, NA_KH * GRID_W)


def _na(qa, ka, va, bias):
    bsz, s, _ = qa.shape
    rows = s // GRID_W
    assert rows >= NA_KH and rows % NA_ROWS_PER_STEP == 0
    tq = NA_ROWS_PER_STEP * GRID_W
    return pl.pallas_call(
        functools.partial(_na_kernel, rows=rows),
        grid=(bsz, A_WIDTH // LANES, s // tq),
        in_specs=[pl.BlockSpec((1, tq, LANES), lambda b, hp, t: (b, t, hp)),
                  pl.BlockSpec((1, s, LANES), lambda b, hp, t: (b, 0, hp)),
                  pl.BlockSpec((1, s, LANES), lambda b, hp, t: (b, 0, hp)),
                  pl.BlockSpec((2, NA_KH, GRID_W, NA_KH * GRID_W), lambda b, hp, t: (hp, 0, 0, 0))],
        out_specs=pl.BlockSpec((1, tq, LANES), lambda b, hp, t: (b, t, hp)),
        out_shape=jax.ShapeDtypeStruct((bsz, s, A_WIDTH), BF16),
        compiler_params=_params(3),
        name="na",
    )(qa, ka, va, bias)


def _attend(streams, st_ref, p_ref):
    nck, _, tk = streams[0][2].shape
    n = streams[0][0].shape[1]

    def scores(i, j, par, h):
        q, k_ref, _ = streams[i]
        off = pl.multiple_of(j * tk + h * MXU_TILE, MXU_TILE)
        st = jnp.dot(k_ref[pl.ds(off, MXU_TILE), :], q, preferred_element_type=F32)
        st_ref[par][i, h * MXU_TILE:(h + 1) * MXU_TILE, :] = st
        c = st[0:SUBLANES]
        for r in range(SUBLANES, MXU_TILE, SUBLANES):
            c = jnp.maximum(c, st[r:r + SUBLANES])
        return c

    def softmax(i, par, m, cmax):
        m_new = jnp.maximum(m, jnp.max(cmax, axis=0, keepdims=True))
        for r in range(0, tk, SOFTMAX_SLAB):
            rows = slice(r, r + SOFTMAX_SLAB)
            p_ref[par][i, rows, :] = jnp.exp2(st_ref[par][i, rows, :] - m_new).astype(BF16)
        return m_new, jnp.exp2(m - m_new)

    def step(j, par, carry, with_scores=True, with_softmax=True):
        out = []
        for i, (m, alpha, acc, cmax) in enumerate(carry):
            acc = acc * alpha
            cmax_next = cmax
            for h in range(tk // MXU_TILE):
                if with_scores:
                    c = scores(i, j + 2, (par + 2) % RING, h)
                    cmax_next = c if h == 0 else jnp.maximum(cmax_next, c)
                rows = slice(h * MXU_TILE, (h + 1) * MXU_TILE)
                acc = acc + jnp.dot(streams[i][2][j, :, rows], p_ref[par][i, rows, :],
                                    preferred_element_type=F32)
            if with_softmax:
                m, alpha = softmax(i, (par + 1) % RING, m, cmax)
            out.append((m, alpha, acc, cmax_next))
        return tuple(out)

    def body(jj, carry):
        for u in range(STEPS_PER_TRIP):
            carry = step(STEPS_PER_TRIP * jj + u, u % RING, carry)
        return carry

    def chunk_scores(i, j, par):
        return functools.reduce(jnp.maximum, [scores(i, j, par, h) for h in range(tk // MXU_TILE)])

    carry = []
    for i in range(len(streams)):
        cmax0 = chunk_scores(i, 0, 0)
        cmax1 = chunk_scores(i, 1, 1) if nck > 1 else cmax0
        m, alpha = softmax(i, 0, jnp.full((1, n), -jnp.inf, F32), cmax0)
        carry.append((m, alpha, jnp.zeros((V_ROWS, n), F32), cmax1))
    carry = tuple(carry)
    trips = max(nck - 2, 0) // STEPS_PER_TRIP
    carry = lax.fori_loop(0, trips, body, carry)
    for j in range(STEPS_PER_TRIP * trips, nck):
        carry = step(j, j % RING, carry, with_scores=j + 2 < nck, with_softmax=j + 1 < nck)
    return [acc[0:HEAD_DIM] / acc[HEAD_DIM:HEAD_DIM + 1] for _, _, acc, _ in carry]


def _attend_scratch(n_streams, tk, n):
    return ([pltpu.VMEM((n_streams, tk, n), F32)] * RING + [pltpu.VMEM((n_streams, tk, n), BF16)] * RING)


def _gqa_kernel(q_ref, k_ref, vt_ref, o_ref, *scratch):
    streams = [(q_ref[0, h], k_ref.at[0, 0], vt_ref.at[0, 0]) for h in range(q_ref.shape[1])]
    outs = _attend(streams, scratch[:RING], scratch[RING:])
    o_ref[0] = jnp.concatenate(outs, axis=0).T.astype(BF16)


def _gqa(qb, kb, vt):
    bsz, _, _, s = qb.shape
    tq = GQA_Q_TILE
    group = B_HEADS // B_KV_HEADS
    gw = group * HEAD_DIM
    return pl.pallas_call(
        _gqa_kernel,
        grid=(bsz, B_KV_HEADS, s // tq),
        in_specs=[pl.BlockSpec((1, group, LANES, tq), lambda b, kv, i: (b, kv, 0, i)),
                  pl.BlockSpec((1, 1, s, LANES), lambda b, kv, i: (b, kv, 0, 0)),
                  pl.BlockSpec((1, 1) + vt.shape[2:], lambda b, kv, i: (b, kv, 0, 0, 0))],
        out_specs=pl.BlockSpec((1, tq, gw), lambda b, kv, i: (b, i, kv)),
        out_shape=jax.ShapeDtypeStruct((bsz, s, B_WIDTH), BF16),
        scratch_shapes=_attend_scratch(group, vt.shape[-1], tq),
        compiler_params=_params(3),
        name="gqa",
    )(qb, kb, vt)


def _mla_kernel(q_ref, k_ref, vt_ref, o_ref, *scratch):
    tq = q_ref.shape[3]
    sub = scratch[0].shape[-1]
    streams = [(q_ref[0, hh, :, c * sub:(c + 1) * sub], k_ref.at[0, hh], vt_ref.at[0, hh])
               for hh in range(2) for c in range(tq // sub)]
    outs = _attend(streams, scratch[:RING], scratch[RING:])
    per_head = len(outs) // 2
    ot = jnp.concatenate([jnp.concatenate(outs[hh * per_head:(hh + 1) * per_head], axis=1)
                          for hh in range(2)], axis=0)
    o_ref[0] = ot.T.astype(BF16)


def _mla(q, k, vt):
    bsz, nh, _, s = q.shape
    tq = MLA_Q_TILE
    return pl.pallas_call(
        _mla_kernel,
        grid=(bsz, nh // 2, s // tq),
        in_specs=[pl.BlockSpec((1, 2, LANES, tq), lambda b, hp, i: (b, hp, 0, i)),
                  pl.BlockSpec((1, 2, s, LANES), lambda b, hp, i: (b, hp, 0, 0)),
                  pl.BlockSpec((1, 2) + vt.shape[2:], lambda b, hp, i: (b, hp, 0, 0, 0))],
        out_specs=pl.BlockSpec((1, tq, LANES), lambda b, hp, i: (b, i, hp)),
        out_shape=jax.ShapeDtypeStruct((bsz, s, nh * C_V), BF16),
        scratch_shapes=_attend_scratch(2 * tq // STREAM_QUERIES, vt.shape[-1], STREAM_QUERIES),
        compiler_params=_params(3),
        name="mla",
    )(q, k, vt)


def _out0_kernel(x_ref, ma_ref, mb_ref, gate_ref, w_ref, y_ref):
    sg = _silu(gate_ref[0].astype(F32))
    ga = (ma_ref[0].astype(F32) * sg[:, 0:A_WIDTH]).astype(BF16)
    gb = (mb_ref[0].astype(F32) * sg[:, A_WIDTH:]).astype(BF16)
    y_ref[0] = (x_ref[0] + jnp.dot(ga, w_ref[0:A_WIDTH, :], preferred_element_type=F32)
                + jnp.dot(gb, w_ref[A_WIDTH:, :], preferred_element_type=F32))


def _out0(x, ma, mb, gate, w):
    bsz, s, _ = x.shape
    tm = TOKEN_TILE
    tok = lambda width: pl.BlockSpec((1, tm, width), lambda b, i: (b, i, 0))
    return pl.pallas_call(
        _out0_kernel,
        grid=(bsz, s // tm),
        in_specs=[tok(D_MODEL), tok(A_WIDTH), tok(B_WIDTH), tok(D_MODEL),
                  pl.BlockSpec(w.shape, lambda b, i: (0, 0))],
        out_specs=tok(D_MODEL),
        out_shape=jax.ShapeDtypeStruct(x.shape, F32),
        compiler_params=_params(2),
        name="out0",
    )(x, ma, mb, gate, w)


def _out1_kernel(x_ref, m_ref, gate_ref, w_ref, g_ref, y_ref):
    gm = (m_ref[0].astype(F32) * _silu(gate_ref[0].astype(F32))).astype(BF16)
    y = x_ref[0] + jnp.dot(gm, w_ref[...], preferred_element_type=F32)
    y_ref[0] = _rms(y, g_ref[...])


def _out1(x, m, gate, w, g):
    bsz, s, _ = x.shape
    tm = TOKEN_TILE
    tok = pl.BlockSpec((1, tm, D_MODEL), lambda b, i: (b, i, 0))
    return pl.pallas_call(
        _out1_kernel,
        grid=(bsz, s // tm),
        in_specs=[tok, tok, tok, pl.BlockSpec(w.shape, lambda b, i: (0, 0)),
                  pl.BlockSpec((1, D_MODEL), lambda b, i: (0, 0))],
        out_specs=tok,
        out_shape=jax.ShapeDtypeStruct(x.shape, F32),
        compiler_params=_params(2),
        name="out1",
    )(x, m, gate, w, g)


def _proj1_kernel(x_ref, g_ref, w_ref, qg_ref, kvg_ref, wuq_ref, wuk_ref, wuv_ref, cos_ref, sin_ref,
                  q_ref, k_ref, vt_ref, gate_ref):
    tm = x_ref.shape[1]
    h = _rms(x_ref[0], g_ref[...]).astype(BF16)
    lat_w = C_Q_RANK + C_KV_RANK + LANES
    lat = jnp.dot(h, w_ref[:, 0:lat_w], preferred_element_type=F32)
    gate_ref[0] = jnp.dot(h, w_ref[:, lat_w:], preferred_element_type=F32).astype(BF16)

    cos = cos_ref[...]
    sin = sin_ref[...]
    lane = lax.broadcasted_iota(jnp.int32, (tm, LANES), 1)
    low = lane < C_NOPE + C_ROPE // 2

    def rope(xc):
        partner = jnp.where(low, pltpu.roll(xc, LANES - C_ROPE // 2, 1),
                            pltpu.roll(xc, C_ROPE // 2, 1))
        return xc * cos + partner * sin

    cq = _rms(lat[:, 0:C_Q_RANK], qg_ref[...]).astype(BF16)
    q_all = jnp.dot(cq, wuq_ref[...], preferred_element_type=F32)
    for hd in range(C_HEADS):
        qh = rope(q_all[:, hd * LANES:(hd + 1) * LANES])
        q_ref[0, hd] = (qh * (C_QK_DIM ** -0.5 * LOG2E)).T.astype(BF16)

    ckv = _rms(lat[:, C_Q_RANK:C_Q_RANK + C_KV_RANK], kvg_ref[...]).astype(BF16)
    k_rope = rope(lat[:, C_Q_RANK + C_KV_RANK:lat_w])
    k_all = jnp.dot(ckv, wuk_ref[...], preferred_element_type=F32)
    for hd in range(C_HEADS):
        k_ref[0, hd] = (k_all[:, hd * LANES:(hd + 1) * LANES] + k_rope).astype(BF16)

    vt = jnp.dot(ckv, wuv_ref[...], preferred_element_type=F32).T
    ones = jnp.ones((V_ROWS - C_V, tm), BF16)
    for hd in range(C_HEADS):
        vt_ref[0, hd, 0, 0:C_V, :] = vt[hd * C_V:(hd + 1) * C_V].astype(BF16)
        vt_ref[0, hd, 0, C_V:V_ROWS, :] = ones


def _proj1(x, g, w, qg, kvg, wuq, wuk, wuv, cos, sin):
    bsz, s, _ = x.shape
    tm = TOKEN_TILE
    nck = s // tm
    const = lambda shape: pl.BlockSpec(shape, lambda b, i: (0,) * len(shape))
    return pl.pallas_call(
        _proj1_kernel,
        grid=(bsz, nck),
        in_specs=[pl.BlockSpec((1, tm, D_MODEL), lambda b, i: (b, i, 0)),
                  const((1, D_MODEL)), const(w.shape), const((1, C_Q_RANK)), const((1, C_KV_RANK)),
                  const(wuq.shape), const(wuk.shape), const(wuv.shape),
                  pl.BlockSpec((tm, LANES), lambda b, i: (i, 0)),
                  pl.BlockSpec((tm, LANES), lambda b, i: (i, 0))],
        out_specs=[pl.BlockSpec((1, C_HEADS, LANES, tm), lambda b, i: (b, 0, 0, i)),
                   pl.BlockSpec((1, C_HEADS, tm, LANES), lambda b, i: (b, 0, i, 0)),
                   pl.BlockSpec((1, C_HEADS, 1, V_ROWS, tm), lambda b, i: (b, 0, i, 0, 0)),
                   pl.BlockSpec((1, tm, D_MODEL), lambda b, i: (b, i, 0))],
        out_shape=[jax.ShapeDtypeStruct((bsz, C_HEADS, LANES, s), BF16),
                   jax.ShapeDtypeStruct((bsz, C_HEADS, s, LANES), BF16),
                   jax.ShapeDtypeStruct((bsz, C_HEADS, nck, V_ROWS, tm), BF16),
                   jax.ShapeDtypeStruct((bsz, s, D_MODEL), BF16)],
        compiler_params=_params(2),
        name="proj1",
    )(x, g, w, qg, kvg, wuq, wuk, wuv, cos, sin)


def _axial_angles(n_tok, rot_dim):
    n_freq = rot_dim // 4
    inv = ROPE_THETA ** (-jnp.arange(n_freq, dtype=F32) / n_freq)
    t = jnp.arange(n_tok, dtype=jnp.int32)
    row = (t // GRID_W).astype(F32)
    col = (t % GRID_W).astype(F32)
    ang = jnp.concatenate([row[:, None] * inv[None], col[:, None] * inv[None]], axis=-1)
    return jnp.cos(ang), jnp.sin(ang)


def _rope_tables(s):
    c, sn = _axial_angles(s, HEAD_DIM)
    cos0 = jnp.tile(jnp.concatenate([c, c], axis=-1), (1, LANES // HEAD_DIM))
    sin0 = jnp.tile(jnp.concatenate([-sn, sn], axis=-1), (1, LANES // HEAD_DIM))
    c, sn = _axial_angles(s, C_ROPE)
    pad = LANES - C_QK_DIM
    cos1 = jnp.concatenate([jnp.ones((s, C_NOPE), F32), c, c, jnp.ones((s, pad), F32)], axis=-1)
    sin1 = jnp.concatenate([jnp.zeros((s, C_NOPE), F32), -sn, sn, jnp.zeros((s, pad), F32)], axis=-1)
    return cos0, sin0, cos1, sin1


def _prepare(norm_e, w_in_e, rpb_a, qnorm_b, knorm_b, w_out_e,
             norm_o, w_in_o, qlat_g, kvlat_g, w_uq, w_ukv, w_out_o, norm_f):
    rep = LANES // HEAD_DIM
    lat = C_Q_RANK + C_KV_RANK
    z = lambda n: jnp.zeros((D_MODEL, n), F32)
    w1 = jnp.concatenate([w_in_o[0][:, :lat], z(C_NOPE), w_in_o[0][:, lat:lat + C_ROPE],
                          z(LANES - C_QK_DIM), w_in_o[0][:, lat + C_ROPE:]], axis=1)
    wuq = jnp.pad(w_uq[0].reshape(C_Q_RANK, C_HEADS, C_QK_DIM),
                  ((0, 0), (0, 0), (0, LANES - C_QK_DIM))).reshape(C_Q_RANK, C_HEADS * LANES)
    wkv = w_ukv[0].reshape(C_KV_RANK, C_HEADS, C_NOPE + C_V)
    wuk = jnp.pad(wkv[:, :, :C_NOPE], ((0, 0), (0, 0), (0, LANES - C_NOPE))).reshape(C_KV_RANK, C_HEADS * LANES)
    wuv = wkv[:, :, C_NOPE:].reshape(C_KV_RANK, C_WIDTH)
    return dict(
        norm_e=norm_e[0][None], w_in_e=w_in_e[0].astype(BF16), bias=_na_bias_table(rpb_a[0]),
        qg=jnp.tile(qnorm_b[0], rep)[None], kg=jnp.tile(knorm_b[0], rep)[None],
        w_out_e=w_out_e[0].astype(BF16),
        norm_o=norm_o[0][None], w1=w1.astype(BF16), qlat_g=qlat_g[0][None], kvlat_g=kvlat_g[0][None],
        wuq=wuq.astype(BF16), wuk=wuk.astype(BF16), wuv=wuv.astype(BF16),
        w_out_o=w_out_o[0].astype(BF16), norm_f=norm_f[None])


def _trunk(x, p):
    cos0, sin0, cos1, sin1 = _rope_tables(x.shape[1])
    qa, ka, va, qb, kb, vbt, gate0 = _proj0(x, p["norm_e"], p["w_in_e"], cos0, sin0, p["qg"], p["kg"])
    mix_a = _na(qa, ka, va, p["bias"])
    mix_b = _gqa(qb, kb, vbt)
    x1 = _out0(x, mix_a, mix_b, gate0, p["w_out_e"])
    q, k, vt, gate1 = _proj1(x1, p["norm_o"], p["w1"], p["qlat_g"], p["kvlat_g"],
                             p["wuq"], p["wuk"], p["wuv"], cos1, sin1)
    mix_c = _mla(q, k, vt)
    return _out1(x1, mix_c, gate1, p["w_out_o"], p["norm_f"])


def kernel(x_prompt, x_sample, norm_e, w_in_e, rpb_a, qnorm_b, knorm_b, w_out_e,
           norm_o, w_in_o, qlat_g, kvlat_g, w_uq, w_ukv, w_out_o, norm_f):
    assert norm_e.shape[0] == 1 and norm_o.shape[0] == 1
    p = _prepare(norm_e, w_in_e, rpb_a, qnorm_b, knorm_b, w_out_e,
                 norm_o, w_in_o, qlat_g, kvlat_g, w_uq, w_ukv, w_out_o, norm_f)
    return (_trunk(x_prompt, p), _trunk(x_sample, p))
```

```python
import functools

import numpy as np
import jax
import jax.numpy as jnp
from jax import lax
from jax.experimental import pallas as pl
from jax.experimental.pallas import tpu as pltpu

D_MODEL = 1024
GRID_W = 64
HEAD_DIM = 64
A_HEADS = 8
NA_KH = 8
NA_KW = 16
B_HEADS = 8
B_KV_HEADS = 2
C_HEADS = 16
C_NOPE = 64
C_ROPE = 32
C_V = 64
C_Q_RANK = 384
C_KV_RANK = 256
ROPE_THETA = 10000.0
EPS = 1e-6

A_WIDTH = A_HEADS * HEAD_DIM
B_WIDTH = B_HEADS * HEAD_DIM
B_KV_WIDTH = B_KV_HEADS * HEAD_DIM
C_WIDTH = C_HEADS * C_V
C_QK_DIM = C_NOPE + C_ROPE

LANES = 128
SUBLANES = 8
MXU_TILE = 256
SOFTMAX_SLAB = 32
TOKEN_TILE = 512
V_ROWS = HEAD_DIM + 16
NA_ROWS_PER_STEP = 8
NA_GROUP = 8
STREAM_QUERIES = 256
GQA_Q_TILE = STREAM_QUERIES
MLA_Q_TILE = 2 * STREAM_QUERIES
RING = 3
STEPS_PER_BLOCK = 32
STEPS_PER_TRIP = 30
LOG2E = 1.4426950408889634
VMEM_LIMIT = 56 * 1024 * 1024
MASKED = -1e30

F32 = jnp.float32
BF16 = jnp.bfloat16
_NT = (((1,), (1,)), ((), ()))


def _params(n_axes):
    return pltpu.CompilerParams(dimension_semantics=("arbitrary",) * n_axes,
                                vmem_limit_bytes=VMEM_LIMIT)


def _rms(x, g):
    return x * lax.rsqrt(jnp.mean(x * x, axis=-1, keepdims=True) + EPS) * g


def _silu(g):
    return g * jax.nn.sigmoid(g)


def _proj0_kernel(x_ref, g_ref, w_ref, cos_ref, sin_ref, qg_ref, kg_ref,
                  qa_ref, ka_ref, va_ref, qb_ref, kb_ref, vt_ref, gate_ref):
    tm = x_ref.shape[1]
    h = _rms(x_ref[0], g_ref[...]).astype(BF16)

    a = jnp.dot(h, w_ref[:, 0:3 * A_WIDTH], preferred_element_type=F32)
    qa_ref[0] = (a[:, 0:A_WIDTH] * (HEAD_DIM ** -0.5 * LOG2E)).astype(BF16)
    ka_ref[0] = a[:, A_WIDTH:2 * A_WIDTH].astype(BF16)
    va_ref[0] = a[:, 2 * A_WIDTH:3 * A_WIDTH].astype(BF16)

    o = 3 * A_WIDTH
    b = jnp.dot(h, w_ref[:, o:o + B_WIDTH + 2 * B_KV_WIDTH], preferred_element_type=F32)
    cos = cos_ref[...]
    sin = sin_ref[...]
    lane = lax.broadcasted_iota(jnp.int32, (tm, LANES), 1)
    first = lane < HEAD_DIM
    low = (lane % HEAD_DIM) < HEAD_DIM // 2

    def norm_rope(xc, g):
        sq = xc * xc
        sa = jnp.sum(jnp.where(first, sq, 0.0), axis=-1, keepdims=True)
        sb = jnp.sum(jnp.where(first, 0.0, sq), axis=-1, keepdims=True)
        ms = jnp.where(first, sa, sb) * (1.0 / HEAD_DIM)
        y = xc * lax.rsqrt(ms + EPS) * g
        partner = jnp.where(low, pltpu.roll(y, LANES - HEAD_DIM // 2, 1),
                            pltpu.roll(y, HEAD_DIM // 2, 1))
        return y * cos + partner * sin

    qg = qg_ref[...]
    for j in range(B_WIDTH // LANES):
        qj = norm_rope(b[:, j * LANES:(j + 1) * LANES], qg)
        qt = (qj * (HEAD_DIM ** -0.5 * LOG2E)).T.astype(BF16)
        upper = lax.broadcasted_iota(jnp.int32, qt.shape, 0) < HEAD_DIM
        qb_ref[0, 2 * j] = jnp.where(upper, qt, jnp.zeros_like(qt))
        qb_ref[0, 2 * j + 1] = jnp.where(upper, jnp.zeros_like(qt), qt)
    kk = norm_rope(b[:, B_WIDTH:B_WIDTH + LANES], kg_ref[...])
    ks = pltpu.roll(kk, HEAD_DIM, 1)
    kb_ref[0, 0] = jnp.where(first, kk, ks).astype(BF16)
    kb_ref[0, 1] = jnp.where(first, ks, kk).astype(BF16)
    vt = b[:, B_WIDTH + LANES:B_WIDTH + 2 * LANES].T
    ones = jnp.ones((V_ROWS - HEAD_DIM, tm), BF16)
    for kv in range(B_KV_HEADS):
        vt_ref[0, kv, 0, 0:HEAD_DIM, :] = vt[kv * HEAD_DIM:(kv + 1) * HEAD_DIM].astype(BF16)
        vt_ref[0, kv, 0, HEAD_DIM:V_ROWS, :] = ones

    o += B_WIDTH + 2 * B_KV_WIDTH
    gate_ref[0] = jnp.dot(h, w_ref[:, o:o + D_MODEL], preferred_element_type=F32).astype(BF16)


def _proj0(x, g, w, cos, sin, qg, kg):
    bsz, s, _ = x.shape
    tm = TOKEN_TILE
    nck = s // tm
    tok = lambda width: pl.BlockSpec((1, tm, width), lambda b, i: (b, i, 0))
    const = lambda shape: pl.BlockSpec(shape, lambda b, i: (0,) * len(shape))
    return pl.pallas_call(
        _proj0_kernel,
        grid=(bsz, nck),
        in_specs=[tok(D_MODEL), const((1, D_MODEL)), const(w.shape),
                  pl.BlockSpec((tm, LANES), lambda b, i: (i, 0)),
                  pl.BlockSpec((tm, LANES), lambda b, i: (i, 0)),
                  const((1, LANES)), const((1, LANES))],
        out_specs=[tok(A_WIDTH), tok(A_WIDTH), tok(A_WIDTH),
                   pl.BlockSpec((1, B_HEADS, LANES, tm), lambda b, i: (b, 0, 0, i)),
                   pl.BlockSpec((1, B_KV_HEADS, tm, LANES), lambda b, i: (b, 0, i, 0)),
                   pl.BlockSpec((1, B_KV_HEADS, 1, V_ROWS, tm), lambda b, i: (b, 0, i, 0, 0)),
                   tok(D_MODEL)],
        out_shape=[jax.ShapeDtypeStruct((bsz, s, A_WIDTH), BF16)] * 3
        + [jax.ShapeDtypeStruct((bsz, B_HEADS, LANES, s), BF16),
           jax.ShapeDtypeStruct((bsz, B_KV_HEADS, s, LANES), BF16),
           jax.ShapeDtypeStruct((bsz, B_KV_HEADS, nck, V_ROWS, tm), BF16),
           jax.ShapeDtypeStruct((bsz, s, D_MODEL), BF16)],
        compiler_params=_params(2),
        name="proj0",
    )(x, g, w, cos, sin, qg, kg)


def _na_kernel(q_ref, k_ref, v_ref, bias_ref, o_ref, *, rows):
    t = pl.program_id(2)
    lane = lax.broadcasted_iota(jnp.int32, (GRID_W, LANES), 1)
    first = lane < HEAD_DIM
    win = NA_KH * GRID_W

    def body(g, carry):
        offs, scores = [], []
        for u in range(NA_GROUP):
            i = g * NA_GROUP + u
            r = t * NA_ROWS_PER_STEP + i
            rs = jnp.clip(r - NA_KH // 2, 0, rows - NA_KH)
            case = r - rs
            qoff = pl.multiple_of(i * GRID_W, GRID_W)
            koff = pl.multiple_of(rs * GRID_W, GRID_W)
            offs.append((qoff, koff))
            q = q_ref[0, pl.ds(qoff, GRID_W), :]
            kw = k_ref[0, pl.ds(koff, win), :]
            for hh in range(2):
                qm = jnp.where(first if hh == 0 else jnp.logical_not(first), q, jnp.zeros_like(q))
                scores.append(lax.dot_general(qm, kw, _NT, preferred_element_type=F32)
                              + bias_ref[hh, case])
        probs = []
        for sc in scores:
            e = jnp.exp2(sc - jnp.max(sc, axis=-1, keepdims=True))
            probs.append((e.astype(BF16), jnp.sum(e, axis=-1, keepdims=True)))
        for u, (qoff, koff) in enumerate(offs):
            vw = v_ref[0, pl.ds(koff, win), :]
            o0, o1 = [jnp.dot(e, vw, preferred_element_type=F32) / l for e, l in probs[2 * u:2 * u + 2]]
            o_ref[0, pl.ds(qoff, GRID_W), :] = jnp.where(first, o0, o1).astype(BF16)
        return carry

    lax.fori_loop(0, NA_ROWS_PER_STEP // NA_GROUP, body, 0)


def _na_bias_table(rpb):
    qc = np.arange(GRID_W)[:, None]
    kc = np.arange(GRID_W)[None, :]
    cs = np.clip(qc - NA_KW // 2, 0, GRID_W - NA_KW)
    valid = (kc >= cs) & (kc < cs + NA_KW)
    onehot = ((kc - qc + NA_KW - 1)[..., None] == np.arange(2 * NA_KW - 1)) & valid[..., None]
    cols = jnp.einsum("hrd,qkd->hrqk", rpb * LOG2E, jnp.asarray(onehot, F32),
                      precision=lax.Precision.HIGHEST)
    cols = jnp.where(valid[None, None], cols, MASKED)
    tbl = jnp.stack([cols[:, NA_KH - 1 - c:2 * NA_KH - 1 - c] for c in range(NA_KH)], axis=1)
    return tbl.transpose(0, 1, 3, 2, 4).reshape(rpb.shape[0], NA_KH, GRID_W, NA_KH * GRID_W)


def _na(qa, ka, va, bias):
    bsz, s, _ = qa.shape
    rows = s // GRID_W
    assert rows >= NA_KH and rows % NA_ROWS_PER_STEP == 0
    tq = NA_ROWS_PER_STEP * GRID_W
    return pl.pallas_call(
        functools.partial(_na_kernel, rows=rows),
        grid=(bsz, A_WIDTH // LANES, s // tq),
        in_specs=[pl.BlockSpec((1, tq, LANES), lambda b, hp, t: (b, t, hp)),
                  pl.BlockSpec((1, s, LANES), lambda b, hp, t: (b, 0, hp)),
                  pl.BlockSpec((1, s, LANES), lambda b, hp, t: (b, 0, hp)),
                  pl.BlockSpec((2, NA_KH, GRID_W, NA_KH * GRID_W), lambda b, hp, t: (hp, 0, 0, 0))],
        out_specs=pl.BlockSpec((1, tq, LANES), lambda b, hp, t: (b, t, hp)),
        out_shape=jax.ShapeDtypeStruct((bsz, s, A_WIDTH), BF16),
        compiler_params=_params(3),
        name="na",
    )(qa, ka, va, bias)


def _attend(streams, st_ref, p_ref):
    nck, _, tk = streams[0][2].shape
    n = streams[0][0].shape[1]

    def scores(i, j, par, h):
        q, k_ref, _ = streams[i]
        off = pl.multiple_of(j * tk + h * MXU_TILE, MXU_TILE)
        st = jnp.dot(k_ref[pl.ds(off, MXU_TILE), :], q, preferred_element_type=F32)
        st_ref[par][i, h * MXU_TILE:(h + 1) * MXU_TILE, :] = st
        c = st[0:SUBLANES]
        for r in range(SUBLANES, MXU_TILE, SUBLANES):
            c = jnp.maximum(c, st[r:r + SUBLANES])
        return c

    def softmax(i, par, m, cmax):
        m_new = jnp.maximum(m, jnp.max(cmax, axis=0, keepdims=True))
        for r in range(0, tk, SOFTMAX_SLAB):
            rows = slice(r, r + SOFTMAX_SLAB)
            p_ref[par][i, rows, :] = jnp.exp2(st_ref[par][i, rows, :] - m_new).astype(BF16)
        return m_new, jnp.exp2(m - m_new)

    def step(j, par, carry, with_scores=True, with_softmax=True):
        out = []
        for i, (m, alpha, acc, cmax) in enumerate(carry):
            acc = acc * alpha
            cmax_next = cmax
            for h in range(tk // MXU_TILE):
                if with_scores:
                    c = scores(i, j + 2, (par + 2) % RING, h)
                    cmax_next = c if h == 0 else jnp.maximum(cmax_next, c)
                rows = slice(h * MXU_TILE, (h + 1) * MXU_TILE)
                acc = acc + jnp.dot(streams[i][2][j, :, rows], p_ref[par][i, rows, :],
                                    preferred_element_type=F32)
            if with_softmax:
                m, alpha = softmax(i, (par + 1) % RING, m, cmax)
            out.append((m, alpha, acc, cmax_next))
        return tuple(out)

    def body(jj, carry):
        for u in range(STEPS_PER_TRIP):
            carry = step(STEPS_PER_TRIP * jj + u, u % RING, carry)
        return carry

    def chunk_scores(i, j, par):
        return functools.reduce(jnp.maximum, [scores(i, j, par, h) for h in range(tk // MXU_TILE)])

    carry = []
    for i in range(len(streams)):
        cmax0 = chunk_scores(i, 0, 0)
        cmax1 = chunk_scores(i, 1, 1) if nck > 1 else cmax0
        m, alpha = softmax(i, 0, jnp.full((1, n), -jnp.inf, F32), cmax0)
        carry.append((m, alpha, jnp.zeros((V_ROWS, n), F32), cmax1))
    carry = tuple(carry)
    trips = max(nck - 2, 0) // STEPS_PER_TRIP
    carry = lax.fori_loop(0, trips, body, carry)
    for j in range(STEPS_PER_TRIP * trips, nck):
        carry = step(j, j % RING, carry, with_scores=j + 2 < nck, with_softmax=j + 1 < nck)
    return [acc[0:HEAD_DIM] / acc[HEAD_DIM:HEAD_DIM + 1] for _, _, acc, _ in carry]


def _attend_scratch(n_streams, tk, n):
    return ([pltpu.VMEM((n_streams, tk, n), F32)] * RING + [pltpu.VMEM((n_streams, tk, n), BF16)] * RING)


def _tiles_per_step(nck, n_tiles):
    t = min(max(1, STEPS_PER_BLOCK // nck), n_tiles)
    while n_tiles % t:
        t -= 1
    return t


def _attend_tiles(tile_streams, scratch):
    sets = len(scratch) // (2 * RING)
    outs = []
    for t, streams in enumerate(tile_streams):
        base = (t % sets) * 2 * RING
        outs.append(_attend(streams, scratch[base:base + RING], scratch[base + RING:base + 2 * RING]))
    return outs


def _gqa_kernel(q_ref, k_ref, vt_ref, o_ref, *scratch):
    tq = scratch[0].shape[-1]
    tiles = q_ref.shape[3] // tq
    outs = _attend_tiles([[(q_ref[0, h, :, t * tq:(t + 1) * tq], k_ref.at[0, 0], vt_ref.at[0, 0])
                           for h in range(q_ref.shape[1])] for t in range(tiles)], scratch)
    for t, o in enumerate(outs):
        o_ref[0, t * tq:(t + 1) * tq, :] = jnp.concatenate(o, axis=0).T.astype(BF16)


def _gqa(qb, kb, vt):
    bsz, _, _, s = qb.shape
    tq = GQA_Q_TILE
    group = B_HEADS // B_KV_HEADS
    gw = group * HEAD_DIM
    tiles = _tiles_per_step(vt.shape[2], s // tq)
    return pl.pallas_call(
        _gqa_kernel,
        grid=(bsz, B_KV_HEADS, s // (tiles * tq)),
        in_specs=[pl.BlockSpec((1, group, LANES, tiles * tq), lambda b, kv, i: (b, kv, 0, i)),
                  pl.BlockSpec((1, 1, s, LANES), lambda b, kv, i: (b, kv, 0, 0)),
                  pl.BlockSpec((1, 1) + vt.shape[2:], lambda b, kv, i: (b, kv, 0, 0, 0))],
        out_specs=pl.BlockSpec((1, tiles * tq, gw), lambda b, kv, i: (b, i, kv)),
        out_shape=jax.ShapeDtypeStruct((bsz, s, B_WIDTH), BF16),
        scratch_shapes=_attend_scratch(group, vt.shape[-1], tq) * min(tiles, 2),
        compiler_params=_params(3),
        name="gqa",
    )(qb, kb, vt)


def _mla_kernel(q_ref, k_ref, vt_ref, o_ref, *scratch):
    sub = scratch[0].shape[-1]
    per_head = scratch[0].shape[0] // 2
    tq = per_head * sub
    tiles = q_ref.shape[3] // tq
    outs = _attend_tiles([[(q_ref[0, hh, :, t * tq + c * sub:t * tq + (c + 1) * sub],
                            k_ref.at[0, hh], vt_ref.at[0, hh])
                           for hh in range(2) for c in range(per_head)] for t in range(tiles)], scratch)
    for t, o in enumerate(outs):
        ot = jnp.concatenate([jnp.concatenate(o[hh * per_head:(hh + 1) * per_head], axis=1)
                              for hh in range(2)], axis=0)
        o_ref[0, t * tq:(t + 1) * tq, :] = ot.T.astype(BF16)


def _mla(q, k, vt):
    bsz, nh, _, s = q.shape
    tq = MLA_Q_TILE
    tiles = _tiles_per_step(vt.shape[2], s // tq)
    return pl.pallas_call(
        _mla_kernel,
        grid=(bsz, nh // 2, s // (tiles * tq)),
        in_specs=[pl.BlockSpec((1, 2, LANES, tiles * tq), lambda b, hp, i: (b, hp, 0, i)),
                  pl.BlockSpec((1, 2, s, LANES), lambda b, hp, i: (b, hp, 0, 0)),
                  pl.BlockSpec((1, 2) + vt.shape[2:], lambda b, hp, i: (b, hp, 0, 0, 0))],
        out_specs=pl.BlockSpec((1, tiles * tq, LANES), lambda b, hp, i: (b, i, hp)),
        out_shape=jax.ShapeDtypeStruct((bsz, s, nh * C_V), BF16),
        scratch_shapes=_attend_scratch(2 * tq // STREAM_QUERIES, vt.shape[-1], STREAM_QUERIES) * min(tiles, 2),
        compiler_params=_params(3),
        name="mla",
    )(q, k, vt)


def _out0_kernel(x_ref, ma_ref, mb_ref, gate_ref, w_ref, y_ref):
    sg = _silu(gate_ref[0].astype(F32))
    ga = (ma_ref[0].astype(F32) * sg[:, 0:A_WIDTH]).astype(BF16)
    gb = (mb_ref[0].astype(F32) * sg[:, A_WIDTH:]).astype(BF16)
    y_ref[0] = (x_ref[0] + jnp.dot(ga, w_ref[0:A_WIDTH, :], preferred_element_type=F32)
                + jnp.dot(gb, w_ref[A_WIDTH:, :], preferred_element_type=F32))


def _out0(x, ma, mb, gate, w):
    bsz, s, _ = x.shape
    tm = TOKEN_TILE
    tok = lambda width: pl.BlockSpec((1, tm, width), lambda b, i: (b, i, 0))
    return pl.pallas_call(
        _out0_kernel,
        grid=(bsz, s // tm),
        in_specs=[tok(D_MODEL), tok(A_WIDTH), tok(B_WIDTH), tok(D_MODEL),
                  pl.BlockSpec(w.shape, lambda b, i: (0, 0))],
        out_specs=tok(D_MODEL),
        out_shape=jax.ShapeDtypeStruct(x.shape, F32),
        compiler_params=_params(2),
        name="out0",
    )(x, ma, mb, gate, w)


def _out1_kernel(x_ref, m_ref, gate_ref, w_ref, g_ref, y_ref):
    gm = (m_ref[0].astype(F32) * _silu(gate_ref[0].astype(F32))).astype(BF16)
    y = x_ref[0] + jnp.dot(gm, w_ref[...], preferred_element_type=F32)
    y_ref[0] = _rms(y, g_ref[...])


def _out1(x, m, gate, w, g):
    bsz, s, _ = x.shape
    tm = TOKEN_TILE
    tok = pl.BlockSpec((1, tm, D_MODEL), lambda b, i: (b, i, 0))
    return pl.pallas_call(
        _out1_kernel,
        grid=(bsz, s // tm),
        in_specs=[tok, tok, tok, pl.BlockSpec(w.shape, lambda b, i: (0, 0)),
                  pl.BlockSpec((1, D_MODEL), lambda b, i: (0, 0))],
        out_specs=tok,
        out_shape=jax.ShapeDtypeStruct(x.shape, F32),
        compiler_params=_params(2),
        name="out1",
    )(x, m, gate, w, g)


def _proj1_kernel(x_ref, g_ref, w_ref, qg_ref, kvg_ref, wuq_ref, wuk_ref, wuv_ref, cos_ref, sin_ref,
                  q_ref, k_ref, vt_ref, gate_ref):
    tm = x_ref.shape[1]
    h = _rms(x_ref[0], g_ref[...]).astype(BF16)
    lat_w = C_Q_RANK + C_KV_RANK + LANES
    lat = jnp.dot(h, w_ref[:, 0:lat_w], preferred_element_type=F32)
    gate_ref[0] = jnp.dot(h, w_ref[:, lat_w:], preferred_element_type=F32).astype(BF16)

    cos = cos_ref[...]
    sin = sin_ref[...]
    lane = lax.broadcasted_iota(jnp.int32, (tm, LANES), 1)
    low = lane < C_NOPE + C_ROPE // 2

    def rope(xc):
        partner = jnp.where(low, pltpu.roll(xc, LANES - C_ROPE // 2, 1),
                            pltpu.roll(xc, C_ROPE // 2, 1))
        return xc * cos + partner * sin

    cq = _rms(lat[:, 0:C_Q_RANK], qg_ref[...]).astype(BF16)
    q_all = jnp.dot(cq, wuq_ref[...], preferred_element_type=F32)
    for hd in range(C_HEADS):
        qh = rope(q_all[:, hd * LANES:(hd + 1) * LANES])
        q_ref[0, hd] = (qh * (C_QK_DIM ** -0.5 * LOG2E)).T.astype(BF16)

    ckv = _rms(lat[:, C_Q_RANK:C_Q_RANK + C_KV_RANK], kvg_ref[...]).astype(BF16)
    k_rope = rope(lat[:, C_Q_RANK + C_KV_RANK:lat_w])
    k_all = jnp.dot(ckv, wuk_ref[...], preferred_element_type=F32)
    for hd in range(C_HEADS):
        k_ref[0, hd] = (k_all[:, hd * LANES:(hd + 1) * LANES] + k_rope).astype(BF16)

    vt = jnp.dot(ckv, wuv_ref[...], preferred_element_type=F32).T
    ones = jnp.ones((V_ROWS - C_V, tm), BF16)
    for hd in range(C_HEADS):
        vt_ref[0, hd, 0, 0:C_V, :] = vt[hd * C_V:(hd + 1) * C_V].astype(BF16)
        vt_ref[0, hd, 0, C_V:V_ROWS, :] = ones


def _proj1(x, g, w, qg, kvg, wuq, wuk, wuv, cos, sin):
    bsz, s, _ = x.shape
    tm = TOKEN_TILE
    nck = s // tm
    const = lambda shape: pl.BlockSpec(shape, lambda b, i: (0,) * len(shape))
    return pl.pallas_call(
        _proj1_kernel,
        grid=(bsz, nck),
        in_specs=[pl.BlockSpec((1, tm, D_MODEL), lambda b, i: (b, i, 0)),
                  const((1, D_MODEL)), const(w.shape), const((1, C_Q_RANK)), const((1, C_KV_RANK)),
                  const(wuq.shape), const(wuk.shape), const(wuv.shape),
                  pl.BlockSpec((tm, LANES), lambda b, i: (i, 0)),
                  pl.BlockSpec((tm, LANES), lambda b, i: (i, 0))],
        out_specs=[pl.BlockSpec((1, C_HEADS, LANES, tm), lambda b, i: (b, 0, 0, i)),
                   pl.BlockSpec((1, C_HEADS, tm, LANES), lambda b, i: (b, 0, i, 0)),
                   pl.BlockSpec((1, C_HEADS, 1, V_ROWS, tm), lambda b, i: (b, 0, i, 0, 0)),
                   pl.BlockSpec((1, tm, D_MODEL), lambda b, i: (b, i, 0))],
        out_shape=[jax.ShapeDtypeStruct((bsz, C_HEADS, LANES, s), BF16),
                   jax.ShapeDtypeStruct((bsz, C_HEADS, s, LANES), BF16),
                   jax.ShapeDtypeStruct((bsz, C_HEADS, nck, V_ROWS, tm), BF16),
                   jax.ShapeDtypeStruct((bsz, s, D_MODEL), BF16)],
        compiler_params=_params(2),
        name="proj1",
    )(x, g, w, qg, kvg, wuq, wuk, wuv, cos, sin)


def _axial_angles(n_tok, rot_dim):
    n_freq = rot_dim // 4
    inv = ROPE_THETA ** (-jnp.arange(n_freq, dtype=F32) / n_freq)
    t = jnp.arange(n_tok, dtype=jnp.int32)
    row = (t // GRID_W).astype(F32)
    col = (t % GRID_W).astype(F32)
    ang = jnp.concatenate([row[:, None] * inv[None], col[:, None] * inv[None]], axis=-1)
    return jnp.cos(ang), jnp.sin(ang)


def _rope_tables(s):
    c, sn = _axial_angles(s, HEAD_DIM)
    cos0 = jnp.tile(jnp.concatenate([c, c], axis=-1), (1, LANES // HEAD_DIM))
    sin0 = jnp.tile(jnp.concatenate([-sn, sn], axis=-1), (1, LANES // HEAD_DIM))
    c, sn = _axial_angles(s, C_ROPE)
    pad = LANES - C_QK_DIM
    cos1 = jnp.concatenate([jnp.ones((s, C_NOPE), F32), c, c, jnp.ones((s, pad), F32)], axis=-1)
    sin1 = jnp.concatenate([jnp.zeros((s, C_NOPE), F32), -sn, sn, jnp.zeros((s, pad), F32)], axis=-1)
    return cos0, sin0, cos1, sin1


def _prepare(norm_e, w_in_e, rpb_a, qnorm_b, knorm_b, w_out_e,
             norm_o, w_in_o, qlat_g, kvlat_g, w_uq, w_ukv, w_out_o, norm_f):
    rep = LANES // HEAD_DIM
    lat = C_Q_RANK + C_KV_RANK
    z = lambda n: jnp.zeros((D_MODEL, n), F32)
    w1 = jnp.concatenate([w_in_o[0][:, :lat], z(C_NOPE), w_in_o[0][:, lat:lat + C_ROPE],
                          z(LANES - C_QK_DIM), w_in_o[0][:, lat + C_ROPE:]], axis=1)
    wuq = jnp.pad(w_uq[0].reshape(C_Q_RANK, C_HEADS, C_QK_DIM),
                  ((0, 0), (0, 0), (0, LANES - C_QK_DIM))).reshape(C_Q_RANK, C_HEADS * LANES)
    wkv = w_ukv[0].reshape(C_KV_RANK, C_HEADS, C_NOPE + C_V)
    wuk = jnp.pad(wkv[:, :, :C_NOPE], ((0, 0), (0, 0), (0, LANES - C_NOPE))).reshape(C_KV_RANK, C_HEADS * LANES)
    wuv = wkv[:, :, C_NOPE:].reshape(C_KV_RANK, C_WIDTH)
    return dict(
        norm_e=norm_e[0][None], w_in_e=w_in_e[0].astype(BF16), bias=_na_bias_table(rpb_a[0]),
        qg=jnp.tile(qnorm_b[0], rep)[None], kg=jnp.tile(knorm_b[0], rep)[None],
        w_out_e=w_out_e[0].astype(BF16),
        norm_o=norm_o[0][None], w1=w1.astype(BF16), qlat_g=qlat_g[0][None], kvlat_g=kvlat_g[0][None],
        wuq=wuq.astype(BF16), wuk=wuk.astype(BF16), wuv=wuv.astype(BF16),
        w_out_o=w_out_o[0].astype(BF16), norm_f=norm_f[None])


def _trunk(x, p):
    cos0, sin0, cos1, sin1 = _rope_tables(x.shape[1])
    qa, ka, va, qb, kb, vbt, gate0 = _proj0(x, p["norm_e"], p["w_in_e"], cos0, sin0, p["qg"], p["kg"])
    mix_a = _na(qa, ka, va, p["bias"])
    mix_b = _gqa(qb, kb, vbt)
    x1 = _out0(x, mix_a, mix_b, gate0, p["w_out_e"])
    q, k, vt, gate1 = _proj1(x1, p["norm_o"], p["w1"], p["qlat_g"], p["kvlat_g"],
                             p["wuq"], p["wuk"], p["wuv"], cos1, sin1)
    mix_c = _mla(q, k, vt)
    return _out1(x1, mix_c, gate1, p["w_out_o"], p["norm_f"])


def kernel(x_prompt, x_sample, norm_e, w_in_e, rpb_a, qnorm_b, knorm_b, w_out_e,
           norm_o, w_in_o, qlat_g, kvlat_g, w_uq, w_ukv, w_out_o, norm_f):
    assert norm_e.shape[0] == 1 and norm_o.shape[0] == 1
    p = _prepare(norm_e, w_in_e, rpb_a, qnorm_b, knorm_b, w_out_e,
                 norm_o, w_in_o, qlat_g, kvlat_g, w_uq, w_ukv, w_out_o, norm_f)
    return (_trunk(x_prompt, p), _trunk(x_sample, p))
```

```python
import functools

import numpy as np
import jax
import jax.numpy as jnp
from jax import lax
from jax.experimental import pallas as pl
from jax.experimental.pallas import tpu as pltpu

D_MODEL = 1024
GRID_W = 64
HEAD_DIM = 64
A_HEADS = 8
NA_KH = 8
NA_KW = 16
B_HEADS = 8
B_KV_HEADS = 2
C_HEADS = 16
C_NOPE = 64
C_ROPE = 32
C_V = 64
C_Q_RANK = 384
C_KV_RANK = 256
ROPE_THETA = 10000.0
EPS = 1e-6

A_WIDTH = A_HEADS * HEAD_DIM
B_WIDTH = B_HEADS * HEAD_DIM
B_KV_WIDTH = B_KV_HEADS * HEAD_DIM
C_WIDTH = C_HEADS * C_V
C_QK_DIM = C_NOPE + C_ROPE

LANES = 128
SUBLANES = 8
MXU_TILE = 256
SOFTMAX_SLAB = 32
TOKEN_TILE = 512
V_ROWS = HEAD_DIM + 16
NA_ROWS_PER_STEP = 8
NA_GROUP = 8
STREAM_QUERIES = 256
GQA_Q_TILE = STREAM_QUERIES
MLA_Q_TILE = 2 * STREAM_QUERIES
RING = 3
STEPS_PER_BLOCK = 32
STEPS_PER_TRIP = 30
LOG2E = 1.4426950408889634
V7X_VMEM_BYTES = 64 * 1024 * 1024
VMEM_LIMIT = V7X_VMEM_BYTES * 7 // 8
MASKED = -1e30

F32 = jnp.float32
BF16 = jnp.bfloat16
_NT = (((1,), (1,)), ((), ()))


def _params(n_axes):
    return pltpu.CompilerParams(dimension_semantics=("arbitrary",) * n_axes,
                                vmem_limit_bytes=VMEM_LIMIT)


def _rms(x, g):
    return x * lax.rsqrt(jnp.mean(x * x, axis=-1, keepdims=True) + EPS) * g


def _silu(g):
    return g * jax.nn.sigmoid(g)


def _proj0_kernel(x_ref, g_ref, w_ref, cos_ref, sin_ref, qg_ref, kg_ref,
                  qa_ref, ka_ref, va_ref, qb_ref, kb_ref, vt_ref, gate_ref):
    tm = x_ref.shape[1]
    h = _rms(x_ref[0], g_ref[...]).astype(BF16)

    a = jnp.dot(h, w_ref[:, 0:3 * A_WIDTH], preferred_element_type=F32)
    qa_ref[0] = (a[:, 0:A_WIDTH] * (HEAD_DIM ** -0.5 * LOG2E)).astype(BF16)
    ka_ref[0] = a[:, A_WIDTH:2 * A_WIDTH].astype(BF16)
    va_ref[0] = a[:, 2 * A_WIDTH:3 * A_WIDTH].astype(BF16)

    o = 3 * A_WIDTH
    b = jnp.dot(h, w_ref[:, o:o + B_WIDTH + 2 * B_KV_WIDTH], preferred_element_type=F32)
    cos = cos_ref[...]
    sin = sin_ref[...]
    lane = lax.broadcasted_iota(jnp.int32, (tm, LANES), 1)
    first = lane < HEAD_DIM
    low = (lane % HEAD_DIM) < HEAD_DIM // 2

    def norm_rope(xc, g):
        sq = xc * xc
        sa = jnp.sum(jnp.where(first, sq, 0.0), axis=-1, keepdims=True)
        sb = jnp.sum(jnp.where(first, 0.0, sq), axis=-1, keepdims=True)
        ms = jnp.where(first, sa, sb) * (1.0 / HEAD_DIM)
        y = xc * lax.rsqrt(ms + EPS) * g
        partner = jnp.where(low, pltpu.roll(y, LANES - HEAD_DIM // 2, 1),
                            pltpu.roll(y, HEAD_DIM // 2, 1))
        return y * cos + partner * sin

    qg = qg_ref[...]
    for j in range(B_WIDTH // LANES):
        qj = norm_rope(b[:, j * LANES:(j + 1) * LANES], qg)
        qt = (qj * (HEAD_DIM ** -0.5 * LOG2E)).T.astype(BF16)
        upper = lax.broadcasted_iota(jnp.int32, qt.shape, 0) < HEAD_DIM
        qb_ref[0, 2 * j] = jnp.where(upper, qt, jnp.zeros_like(qt))
        qb_ref[0, 2 * j + 1] = jnp.where(upper, jnp.zeros_like(qt), qt)
    kk = norm_rope(b[:, B_WIDTH:B_WIDTH + LANES], kg_ref[...])
    ks = pltpu.roll(kk, HEAD_DIM, 1)
    kb_ref[0, 0] = jnp.where(first, kk, ks).astype(BF16)
    kb_ref[0, 1] = jnp.where(first, ks, kk).astype(BF16)
    vt = b[:, B_WIDTH + LANES:B_WIDTH + 2 * LANES].T
    ones = jnp.ones((V_ROWS - HEAD_DIM, tm), BF16)
    for kv in range(B_KV_HEADS):
        vt_ref[0, kv, 0, 0:HEAD_DIM, :] = vt[kv * HEAD_DIM:(kv + 1) * HEAD_DIM].astype(BF16)
        vt_ref[0, kv, 0, HEAD_DIM:V_ROWS, :] = ones

    o += B_WIDTH + 2 * B_KV_WIDTH
    gate_ref[0] = jnp.dot(h, w_ref[:, o:o + D_MODEL], preferred_element_type=F32).astype(BF16)


def _proj0(x, g, w, cos, sin, qg, kg):
    bsz, s, _ = x.shape
    tm = TOKEN_TILE
    nck = s // tm
    tok = lambda width: pl.BlockSpec((1, tm, width), lambda b, i: (b, i, 0))
    const = lambda shape: pl.BlockSpec(shape, lambda b, i: (0,) * len(shape))
    return pl.pallas_call(
        _proj0_kernel,
        grid=(bsz, nck),
        in_specs=[tok(D_MODEL), const((1, D_MODEL)), const(w.shape),
                  pl.BlockSpec((tm, LANES), lambda b, i: (i, 0)),
                  pl.BlockSpec((tm, LANES), lambda b, i: (i, 0)),
                  const((1, LANES)), const((1, LANES))],
        out_specs=[tok(A_WIDTH), tok(A_WIDTH), tok(A_WIDTH),
                   pl.BlockSpec((1, B_HEADS, LANES, tm), lambda b, i: (b, 0, 0, i)),
                   pl.BlockSpec((1, B_KV_HEADS, tm, LANES), lambda b, i: (b, 0, i, 0)),
                   pl.BlockSpec((1, B_KV_HEADS, 1, V_ROWS, tm), lambda b, i: (b, 0, i, 0, 0)),
                   tok(D_MODEL)],
        out_shape=[jax.ShapeDtypeStruct((bsz, s, A_WIDTH), BF16)] * 3
        + [jax.ShapeDtypeStruct((bsz, B_HEADS, LANES, s), BF16),
           jax.ShapeDtypeStruct((bsz, B_KV_HEADS, s, LANES), BF16),
           jax.ShapeDtypeStruct((bsz, B_KV_HEADS, nck, V_ROWS, tm), BF16),
           jax.ShapeDtypeStruct((bsz, s, D_MODEL), BF16)],
        compiler_params=_params(2),
        name="proj0",
    )(x, g, w, cos, sin, qg, kg)


def _na_kernel(q_ref, k_ref, v_ref, bias_ref, o_ref, *, rows):
    t = pl.program_id(2)
    lane = lax.broadcasted_iota(jnp.int32, (GRID_W, LANES), 1)
    first = lane < HEAD_DIM
    win = NA_KH * GRID_W

    def body(g, carry):
        offs, scores = [], []
        for u in range(NA_GROUP):
            i = g * NA_GROUP + u
            r = t * NA_ROWS_PER_STEP + i
            rs = jnp.clip(r - NA_KH // 2, 0, rows - NA_KH)
            case = r - rs
            qoff = pl.multiple_of(i * GRID_W, GRID_W)
            koff = pl.multiple_of(rs * GRID_W, GRID_W)
            offs.append((qoff, koff))
            q = q_ref[0, pl.ds(qoff, GRID_W), :]
            kw = k_ref[0, pl.ds(koff, win), :]
            for hh in range(2):
                qm = jnp.where(first if hh == 0 else jnp.logical_not(first), q, jnp.zeros_like(q))
                scores.append(lax.dot_general(qm, kw, _NT, preferred_element_type=F32)
                              + bias_ref[hh, case])
        probs = []
        for sc in scores:
            e = jnp.exp2(sc - jnp.max(sc, axis=-1, keepdims=True))
            probs.append((e.astype(BF16), jnp.sum(e, axis=-1, keepdims=True)))
        for u, (qoff, koff) in enumerate(offs):
            vw = v_ref[0, pl.ds(koff, win), :]
            o0, o1 = [jnp.dot(e, vw, preferred_element_type=F32) / l for e, l in probs[2 * u:2 * u + 2]]
            o_ref[0, pl.ds(qoff, GRID_W), :] = jnp.where(first, o0, o1).astype(BF16)
        return carry

    lax.fori_loop(0, NA_ROWS_PER_STEP // NA_GROUP, body, 0)


def _na_bias_table(rpb):
    qc = np.arange(GRID_W)[:, None]
    kc = np.arange(GRID_W)[None, :]
    cs = np.clip(qc - NA_KW // 2, 0, GRID_W - NA_KW)
    valid = (kc >= cs) & (kc < cs + NA_KW)
    onehot = ((kc - qc + NA_KW - 1)[..., None] == np.arange(2 * NA_KW - 1)) & valid[..., None]
    cols = jnp.einsum("hrd,qkd->hrqk", rpb * LOG2E, jnp.asarray(onehot, F32),
                      precision=lax.Precision.HIGHEST)
    cols = jnp.where(valid[None, None], cols, MASKED)
    tbl = jnp.stack([cols[:, NA_KH - 1 - c:2 * NA_KH - 1 - c] for c in range(NA_KH)], axis=1)
    return tbl.transpose(0, 1, 3, 2, 4).reshape(rpb.shape[0], NA_KH, GRID_W, NA_KH * GRID_W)


def _na(qa, ka, va, bias):
    bsz, s, _ = qa.shape
    rows = s // GRID_W
    assert rows >= NA_KH and rows % NA_ROWS_PER_STEP == 0
    tq = NA_ROWS_PER_STEP * GRID_W
    return pl.pallas_call(
        functools.partial(_na_kernel, rows=rows),
        grid=(bsz, A_WIDTH // LANES, s // tq),
        in_specs=[pl.BlockSpec((1, tq, LANES), lambda b, hp, t: (b, t, hp)),
                  pl.BlockSpec((1, s, LANES), lambda b, hp, t: (b, 0, hp)),
                  pl.BlockSpec((1, s, LANES), lambda b, hp, t: (b, 0, hp)),
                  pl.BlockSpec((2, NA_KH, GRID_W, NA_KH * GRID_W), lambda b, hp, t: (hp, 0, 0, 0))],
        out_specs=pl.BlockSpec((1, tq, LANES), lambda b, hp, t: (b, t, hp)),
        out_shape=jax.ShapeDtypeStruct((bsz, s, A_WIDTH), BF16),
        compiler_params=_params(3),
        name="na",
    )(qa, ka, va, bias)


def _attend(streams, st_ref, p_ref):
    nck, _, tk = streams[0][2].shape
    n = streams[0][0].shape[1]

    def scores(i, j, par, h):
        q, k_ref, _ = streams[i]
        off = pl.multiple_of(j * tk + h * MXU_TILE, MXU_TILE)
        st = jnp.dot(k_ref[pl.ds(off, MXU_TILE), :], q, preferred_element_type=F32)
        st_ref[par][i, h * MXU_TILE:(h + 1) * MXU_TILE, :] = st
        c = st[0:SUBLANES]
        for r in range(SUBLANES, MXU_TILE, SUBLANES):
            c = jnp.maximum(c, st[r:r + SUBLANES])
        return c

    def softmax(i, par, m, cmax):
        m_new = jnp.maximum(m, jnp.max(cmax, axis=0, keepdims=True))
        for r in range(0, tk, SOFTMAX_SLAB):
            rows = slice(r, r + SOFTMAX_SLAB)
            p_ref[par][i, rows, :] = jnp.exp2(st_ref[par][i, rows, :] - m_new).astype(BF16)
        return m_new, jnp.exp2(m - m_new)

    def step(j, par, carry, with_scores=True, with_softmax=True):
        out = []
        for i, (m, alpha, acc, cmax) in enumerate(carry):
            acc = acc * alpha
            cmax_next = cmax
            for h in range(tk // MXU_TILE):
                if with_scores:
                    c = scores(i, j + 2, (par + 2) % RING, h)
                    cmax_next = c if h == 0 else jnp.maximum(cmax_next, c)
                rows = slice(h * MXU_TILE, (h + 1) * MXU_TILE)
                acc = acc + jnp.dot(streams[i][2][j, :, rows], p_ref[par][i, rows, :],
                                    preferred_element_type=F32)
            if with_softmax:
                m, alpha = softmax(i, (par + 1) % RING, m, cmax)
            out.append((m, alpha, acc, cmax_next))
        return tuple(out)

    def body(jj, carry):
        for u in range(STEPS_PER_TRIP):
            carry = step(STEPS_PER_TRIP * jj + u, u % RING, carry)
        return carry

    def chunk_scores(i, j, par):
        return functools.reduce(jnp.maximum, [scores(i, j, par, h) for h in range(tk // MXU_TILE)])

    carry = []
    for i in range(len(streams)):
        cmax0 = chunk_scores(i, 0, 0)
        cmax1 = chunk_scores(i, 1, 1) if nck > 1 else cmax0
        m, alpha = softmax(i, 0, jnp.full((1, n), -jnp.inf, F32), cmax0)
        carry.append((m, alpha, jnp.zeros((V_ROWS, n), F32), cmax1))
    carry = tuple(carry)
    trips = max(nck - 2, 0) // STEPS_PER_TRIP
    carry = lax.fori_loop(0, trips, body, carry)
    for j in range(STEPS_PER_TRIP * trips, nck):
        carry = step(j, j % RING, carry, with_scores=j + 2 < nck, with_softmax=j + 1 < nck)
    return [acc[0:HEAD_DIM] / acc[HEAD_DIM:HEAD_DIM + 1] for _, _, acc, _ in carry]


def _attend_scratch(n_streams, tk, n):
    return ([pltpu.VMEM((n_streams, tk, n), F32)] * RING + [pltpu.VMEM((n_streams, tk, n), BF16)] * RING)


def _tiles_per_step(nck, n_tiles):
    t = min(max(1, STEPS_PER_BLOCK // nck), n_tiles)
    while n_tiles % t:
        t -= 1
    return t


def _attend_tiles(tile_streams, scratch):
    sets = len(scratch) // (2 * RING)
    outs = []
    for t, streams in enumerate(tile_streams):
        base = (t % sets) * 2 * RING
        outs.append(_attend(streams, scratch[base:base + RING], scratch[base + RING:base + 2 * RING]))
    return outs


def _gqa_kernel(q_ref, k_ref, vt_ref, o_ref, *scratch):
    tq = scratch[0].shape[-1]
    tiles = q_ref.shape[3] // tq
    outs = _attend_tiles([[(q_ref[0, h, :, t * tq:(t + 1) * tq], k_ref.at[0, 0], vt_ref.at[0, 0])
                           for h in range(q_ref.shape[1])] for t in range(tiles)], scratch)
    for t, o in enumerate(outs):
        o_ref[0, t * tq:(t + 1) * tq, :] = jnp.concatenate(o, axis=0).T.astype(BF16)


def _gqa(qb, kb, vt):
    bsz, _, _, s = qb.shape
    tq = GQA_Q_TILE
    group = B_HEADS // B_KV_HEADS
    gw = group * HEAD_DIM
    tiles = _tiles_per_step(vt.shape[2], s // tq)
    return pl.pallas_call(
        _gqa_kernel,
        grid=(bsz, B_KV_HEADS, s // (tiles * tq)),
        in_specs=[pl.BlockSpec((1, group, LANES, tiles * tq), lambda b, kv, i: (b, kv, 0, i)),
                  pl.BlockSpec((1, 1, s, LANES), lambda b, kv, i: (b, kv, 0, 0)),
                  pl.BlockSpec((1, 1) + vt.shape[2:], lambda b, kv, i: (b, kv, 0, 0, 0))],
        out_specs=pl.BlockSpec((1, tiles * tq, gw), lambda b, kv, i: (b, i, kv)),
        out_shape=jax.ShapeDtypeStruct((bsz, s, B_WIDTH), BF16),
        scratch_shapes=_attend_scratch(group, vt.shape[-1], tq) * min(tiles, 2),
        compiler_params=_params(3),
        name="gqa",
    )(qb, kb, vt)


def _mla_kernel(q_ref, k_ref, vt_ref, o_ref, *scratch):
    sub = scratch[0].shape[-1]
    per_head = scratch[0].shape[0] // 2
    tq = per_head * sub
    tiles = q_ref.shape[3] // tq
    outs = _attend_tiles([[(q_ref[0, hh, :, t * tq + c * sub:t * tq + (c + 1) * sub],
                            k_ref.at[0, hh], vt_ref.at[0, hh])
                           for hh in range(2) for c in range(per_head)] for t in range(tiles)], scratch)
    for t, o in enumerate(outs):
        ot = jnp.concatenate([jnp.concatenate(o[hh * per_head:(hh + 1) * per_head], axis=1)
                              for hh in range(2)], axis=0)
        o_ref[0, t * tq:(t + 1) * tq, :] = ot.T.astype(BF16)


def _mla(q, k, vt):
    bsz, nh, _, s = q.shape
    tq = MLA_Q_TILE
    tiles = _tiles_per_step(vt.shape[2], s // tq)
    return pl.pallas_call(
        _mla_kernel,
        grid=(bsz, nh // 2, s // (tiles * tq)),
        in_specs=[pl.BlockSpec((1, 2, LANES, tiles * tq), lambda b, hp, i: (b, hp, 0, i)),
                  pl.BlockSpec((1, 2, s, LANES), lambda b, hp, i: (b, hp, 0, 0)),
                  pl.BlockSpec((1, 2) + vt.shape[2:], lambda b, hp, i: (b, hp, 0, 0, 0))],
        out_specs=pl.BlockSpec((1, tiles * tq, LANES), lambda b, hp, i: (b, i, hp)),
        out_shape=jax.ShapeDtypeStruct((bsz, s, nh * C_V), BF16),
        scratch_shapes=_attend_scratch(2 * tq // STREAM_QUERIES, vt.shape[-1], STREAM_QUERIES) * min(tiles, 2),
        compiler_params=_params(3),
        name="mla",
    )(q, k, vt)


def _out1_kernel(x_ref, m_ref, gate_ref, w_ref, g_ref, y_ref):
    gm = (m_ref[0].astype(F32) * _silu(gate_ref[0].astype(F32))).astype(BF16)
    y = x_ref[0] + jnp.dot(gm, w_ref[...], preferred_element_type=F32)
    y_ref[0] = _rms(y, g_ref[...])


def _out1(x, m, gate, w, g):
    bsz, s, _ = x.shape
    tm = TOKEN_TILE
    tok = pl.BlockSpec((1, tm, D_MODEL), lambda b, i: (b, i, 0))
    return pl.pallas_call(
        _out1_kernel,
        grid=(bsz, s // tm),
        in_specs=[tok, tok, tok, pl.BlockSpec(w.shape, lambda b, i: (0, 0)),
                  pl.BlockSpec((1, D_MODEL), lambda b, i: (0, 0))],
        out_specs=tok,
        out_shape=jax.ShapeDtypeStruct(x.shape, F32),
        compiler_params=_params(2),
        name="out1",
    )(x, m, gate, w, g)


def _mid_kernel(x_ref, ma_ref, mb_ref, gate0_ref, wout_ref, g_ref, w_ref, qg_ref, kvg_ref,
                wuqt_ref, wuk_ref, wuvt_ref, cos_ref, sin_ref, cost_ref, sint_ref,
                y_ref, q_ref, k_ref, vt_ref, gate_ref):
    tm = x_ref.shape[1]
    sg = _silu(gate0_ref[0].astype(F32))
    ga = (ma_ref[0].astype(F32) * sg[:, 0:A_WIDTH]).astype(BF16)
    gb = (mb_ref[0].astype(F32) * sg[:, A_WIDTH:]).astype(BF16)
    y = (x_ref[0] + jnp.dot(ga, wout_ref[0:A_WIDTH, :], preferred_element_type=F32)
         + jnp.dot(gb, wout_ref[A_WIDTH:, :], preferred_element_type=F32))
    y_ref[0] = y

    h = _rms(y, g_ref[...]).astype(BF16)
    lat_w = C_Q_RANK + C_KV_RANK + LANES
    lat = jnp.dot(h, w_ref[:, 0:lat_w], preferred_element_type=F32)
    gate_ref[0] = jnp.dot(h, w_ref[:, lat_w:], preferred_element_type=F32).astype(BF16)

    cqt = _rms(lat[:, 0:C_Q_RANK], qg_ref[...]).T.astype(BF16)
    qt_all = jnp.dot(wuqt_ref[...], cqt, preferred_element_type=F32)
    cost = cost_ref[...]
    sint = sint_ref[...]
    half = C_ROPE // 2
    for hd in range(C_HEADS):
        xq = qt_all[hd * LANES:(hd + 1) * LANES]
        partner = jnp.concatenate([xq[:C_NOPE], xq[C_NOPE + half:C_QK_DIM],
                                   xq[C_NOPE:C_NOPE + half], xq[C_QK_DIM:]], axis=0)
        q_ref[0, hd] = ((xq * cost + partner * sint) * (C_QK_DIM ** -0.5 * LOG2E)).astype(BF16)

    ckv = _rms(lat[:, C_Q_RANK:C_Q_RANK + C_KV_RANK], kvg_ref[...])
    kr = lat[:, C_Q_RANK + C_KV_RANK:lat_w]
    low = lax.broadcasted_iota(jnp.int32, (tm, LANES), 1) < C_NOPE + half
    partner = jnp.where(low, pltpu.roll(kr, LANES - half, 1), pltpu.roll(kr, half, 1))
    k_rope = kr * cos_ref[...] + partner * sin_ref[...]
    k_all = jnp.dot(ckv.astype(BF16), wuk_ref[...], preferred_element_type=F32)
    for hd in range(C_HEADS):
        k_ref[0, hd] = (k_all[:, hd * LANES:(hd + 1) * LANES] + k_rope).astype(BF16)

    vt_all = jnp.dot(wuvt_ref[...], ckv.T.astype(BF16), preferred_element_type=F32)
    ones = jnp.ones((V_ROWS - C_V, tm), BF16)
    for hd in range(C_HEADS):
        vt_ref[0, hd, 0, 0:C_V, :] = vt_all[hd * C_V:(hd + 1) * C_V].astype(BF16)
        vt_ref[0, hd, 0, C_V:V_ROWS, :] = ones


def _mid(x, ma, mb, gate0, wout, g, w, qg, kvg, wuqt, wuk, wuvt, cos, sin, cost, sint):
    bsz, s, _ = x.shape
    tm = TOKEN_TILE
    nck = s // tm
    tok = lambda width: pl.BlockSpec((1, tm, width), lambda b, i: (b, i, 0))
    const = lambda a: pl.BlockSpec(a.shape, lambda b, i: (0,) * a.ndim)
    heads = lambda shape, imap: pl.BlockSpec((1, C_HEADS) + shape, imap)
    return pl.pallas_call(
        _mid_kernel,
        grid=(bsz, nck),
        in_specs=[tok(D_MODEL), tok(A_WIDTH), tok(B_WIDTH), tok(D_MODEL), const(wout),
                  const(g), const(w), const(qg), const(kvg), const(wuqt), const(wuk), const(wuvt),
                  pl.BlockSpec((tm, LANES), lambda b, i: (i, 0)),
                  pl.BlockSpec((tm, LANES), lambda b, i: (i, 0)),
                  pl.BlockSpec((LANES, tm), lambda b, i: (0, i)),
                  pl.BlockSpec((LANES, tm), lambda b, i: (0, i))],
        out_specs=[tok(D_MODEL),
                   heads((LANES, tm), lambda b, i: (b, 0, 0, i)),
                   heads((tm, LANES), lambda b, i: (b, 0, i, 0)),
                   heads((1, V_ROWS, tm), lambda b, i: (b, 0, i, 0, 0)),
                   tok(D_MODEL)],
        out_shape=[jax.ShapeDtypeStruct(x.shape, F32),
                   jax.ShapeDtypeStruct((bsz, C_HEADS, LANES, s), BF16),
                   jax.ShapeDtypeStruct((bsz, C_HEADS, s, LANES), BF16),
                   jax.ShapeDtypeStruct((bsz, C_HEADS, nck, V_ROWS, tm), BF16),
                   jax.ShapeDtypeStruct((bsz, s, D_MODEL), BF16)],
        compiler_params=_params(2),
        name="mid",
    )(x, ma, mb, gate0, wout, g, w, qg, kvg, wuqt, wuk, wuvt, cos, sin, cost, sint)


def _axial_angles(n_tok, rot_dim):
    n_freq = rot_dim // 4
    inv = ROPE_THETA ** (-jnp.arange(n_freq, dtype=F32) / n_freq)
    t = jnp.arange(n_tok, dtype=jnp.int32)
    row = (t // GRID_W).astype(F32)
    col = (t % GRID_W).astype(F32)
    ang = jnp.concatenate([row[:, None] * inv[None], col[:, None] * inv[None]], axis=-1)
    return jnp.cos(ang), jnp.sin(ang)


def _rope_tables(s):
    c, sn = _axial_angles(s, HEAD_DIM)
    cos0 = jnp.tile(jnp.concatenate([c, c], axis=-1), (1, LANES // HEAD_DIM))
    sin0 = jnp.tile(jnp.concatenate([-sn, sn], axis=-1), (1, LANES // HEAD_DIM))
    c, sn = _axial_angles(s, C_ROPE)
    pad = LANES - C_QK_DIM
    cos1 = jnp.concatenate([jnp.ones((s, C_NOPE), F32), c, c, jnp.ones((s, pad), F32)], axis=-1)
    sin1 = jnp.concatenate([jnp.zeros((s, C_NOPE), F32), -sn, sn, jnp.zeros((s, pad), F32)], axis=-1)
    return cos0, sin0, cos1, sin1


def _prepare(norm_e, w_in_e, rpb_a, qnorm_b, knorm_b, w_out_e,
             norm_o, w_in_o, qlat_g, kvlat_g, w_uq, w_ukv, w_out_o, norm_f):
    rep = LANES // HEAD_DIM
    lat = C_Q_RANK + C_KV_RANK
    z = lambda n: jnp.zeros((D_MODEL, n), F32)
    w1 = jnp.concatenate([w_in_o[0][:, :lat], z(C_NOPE), w_in_o[0][:, lat:lat + C_ROPE],
                          z(LANES - C_QK_DIM), w_in_o[0][:, lat + C_ROPE:]], axis=1)
    wuq = jnp.pad(w_uq[0].reshape(C_Q_RANK, C_HEADS, C_QK_DIM),
                  ((0, 0), (0, 0), (0, LANES - C_QK_DIM))).reshape(C_Q_RANK, C_HEADS * LANES)
    wkv = w_ukv[0].reshape(C_KV_RANK, C_HEADS, C_NOPE + C_V)
    wuk = jnp.pad(wkv[:, :, :C_NOPE], ((0, 0), (0, 0), (0, LANES - C_NOPE))).reshape(C_KV_RANK, C_HEADS * LANES)
    wuv = wkv[:, :, C_NOPE:].reshape(C_KV_RANK, C_WIDTH)
    return dict(
        norm_e=norm_e[0][None], w_in_e=w_in_e[0].astype(BF16), bias=_na_bias_table(rpb_a[0]),
        qg=jnp.tile(qnorm_b[0], rep)[None], kg=jnp.tile(knorm_b[0], rep)[None],
        w_out_e=w_out_e[0].astype(BF16),
        norm_o=norm_o[0][None], w1=w1.astype(BF16), qlat_g=qlat_g[0][None], kvlat_g=kvlat_g[0][None],
        wuqt=wuq.T.astype(BF16), wuk=wuk.astype(BF16), wuvt=wuv.T.astype(BF16),
        w_out_o=w_out_o[0].astype(BF16), norm_f=norm_f[None])


def _trunk(x, p):
    cos0, sin0, cos1, sin1 = _rope_tables(x.shape[1])
    qa, ka, va, qb, kb, vbt, gate0 = _proj0(x, p["norm_e"], p["w_in_e"], cos0, sin0, p["qg"], p["kg"])
    mix_a = _na(qa, ka, va, p["bias"])
    mix_b = _gqa(qb, kb, vbt)
    x1, q, k, vt, gate1 = _mid(x, mix_a, mix_b, gate0, p["w_out_e"], p["norm_o"], p["w1"],
                               p["qlat_g"], p["kvlat_g"], p["wuqt"], p["wuk"], p["wuvt"],
                               cos1, sin1, cos1.T, sin1.T)
    mix_c = _mla(q, k, vt)
    return _out1(x1, mix_c, gate1, p["w_out_o"], p["norm_f"])


def kernel(x_prompt, x_sample, norm_e, w_in_e, rpb_a, qnorm_b, knorm_b, w_out_e,
           norm_o, w_in_o, qlat_g, kvlat_g, w_uq, w_ukv, w_out_o, norm_f):
    assert norm_e.shape[0] == 1 and norm_o.shape[0] == 1
    p = _prepare(norm_e, w_in_e, rpb_a, qnorm_b, knorm_b, w_out_e,
                 norm_o, w_in_o, qlat_g, kvlat_g, w_uq, w_ukv, w_out_o, norm_f)
    return (_trunk(x_prompt, p), _trunk(x_sample, p))
```

```python
import functools

import numpy as np
import jax
import jax.numpy as jnp
from jax import lax
from jax.experimental import pallas as pl
from jax.experimental.pallas import tpu as pltpu

D_MODEL = 1024
GRID_W = 64
HEAD_DIM = 64
A_HEADS = 8
NA_KH = 8
NA_KW = 16
B_HEADS = 8
B_KV_HEADS = 2
C_HEADS = 16
C_NOPE = 64
C_ROPE = 32
C_V = 64
C_Q_RANK = 384
C_KV_RANK = 256
ROPE_THETA = 10000.0
EPS = 1e-6

A_WIDTH = A_HEADS * HEAD_DIM
B_WIDTH = B_HEADS * HEAD_DIM
B_KV_WIDTH = B_KV_HEADS * HEAD_DIM
C_WIDTH = C_HEADS * C_V
C_QK_DIM = C_NOPE + C_ROPE

LANES = 128
SUBLANES = 8
MXU_TILE = 256
SOFTMAX_SLAB = 32
TOKEN_TILE = 512
V_ROWS = HEAD_DIM + 16
NA_ROWS_PER_STEP = 8
NA_GROUP = 8
STREAM_QUERIES = 256
GQA_Q_TILE = STREAM_QUERIES
MLA_Q_TILE = 2 * STREAM_QUERIES
RING = 4
STEPS_PER_BLOCK = 32
STEPS_PER_TRIP = 32
assert STEPS_PER_TRIP % RING == 0
LOG2E = 1.4426950408889634
V7X_VMEM_BYTES = 64 * 1024 * 1024
VMEM_LIMIT = V7X_VMEM_BYTES * 7 // 8
MASKED = -1e30

F32 = jnp.float32
BF16 = jnp.bfloat16
_NT = (((1,), (1,)), ((), ()))


def _params(n_axes):
    return pltpu.CompilerParams(dimension_semantics=("arbitrary",) * n_axes,
                                vmem_limit_bytes=VMEM_LIMIT)


def _rms(x, g):
    return x * lax.rsqrt(jnp.mean(x * x, axis=-1, keepdims=True) + EPS) * g


def _silu(g):
    return g * jax.nn.sigmoid(g)


def _proj0_kernel(x_ref, g_ref, w_ref, cos_ref, sin_ref, qg_ref, kg_ref,
                  qa_ref, ka_ref, va_ref, qb_ref, kb_ref, vt_ref, gate_ref):
    tm = x_ref.shape[1]
    h = _rms(x_ref[0], g_ref[...]).astype(BF16)

    a = jnp.dot(h, w_ref[:, 0:3 * A_WIDTH], preferred_element_type=F32)
    qa_ref[0] = (a[:, 0:A_WIDTH] * (HEAD_DIM ** -0.5 * LOG2E)).astype(BF16)
    ka_ref[0] = a[:, A_WIDTH:2 * A_WIDTH].astype(BF16)
    va_ref[0] = a[:, 2 * A_WIDTH:3 * A_WIDTH].astype(BF16)

    o = 3 * A_WIDTH
    b = jnp.dot(h, w_ref[:, o:o + B_WIDTH + 2 * B_KV_WIDTH], preferred_element_type=F32)
    cos = cos_ref[...]
    sin = sin_ref[...]
    lane = lax.broadcasted_iota(jnp.int32, (tm, LANES), 1)
    first = lane < HEAD_DIM
    low = (lane % HEAD_DIM) < HEAD_DIM // 2

    def norm_rope(xc, g):
        sq = xc * xc
        sa = jnp.sum(jnp.where(first, sq, 0.0), axis=-1, keepdims=True)
        sb = jnp.sum(jnp.where(first, 0.0, sq), axis=-1, keepdims=True)
        ms = jnp.where(first, sa, sb) * (1.0 / HEAD_DIM)
        y = xc * lax.rsqrt(ms + EPS) * g
        partner = jnp.where(low, pltpu.roll(y, LANES - HEAD_DIM // 2, 1),
                            pltpu.roll(y, HEAD_DIM // 2, 1))
        return y * cos + partner * sin

    qg = qg_ref[...]
    for j in range(B_WIDTH // LANES):
        qj = norm_rope(b[:, j * LANES:(j + 1) * LANES], qg)
        qt = (qj * (HEAD_DIM ** -0.5 * LOG2E)).T.astype(BF16)
        upper = lax.broadcasted_iota(jnp.int32, qt.shape, 0) < HEAD_DIM
        qb_ref[0, 2 * j] = jnp.where(upper, qt, jnp.zeros_like(qt))
        qb_ref[0, 2 * j + 1] = jnp.where(upper, jnp.zeros_like(qt), qt)
    kk = norm_rope(b[:, B_WIDTH:B_WIDTH + LANES], kg_ref[...])
    ks = pltpu.roll(kk, HEAD_DIM, 1)
    kb_ref[0, 0] = jnp.where(first, kk, ks).astype(BF16)
    kb_ref[0, 1] = jnp.where(first, ks, kk).astype(BF16)
    vt = b[:, B_WIDTH + LANES:B_WIDTH + 2 * LANES].T
    ones = jnp.ones((V_ROWS - HEAD_DIM, tm), BF16)
    for kv in range(B_KV_HEADS):
        vt_ref[0, kv, 0, 0:HEAD_DIM, :] = vt[kv * HEAD_DIM:(kv + 1) * HEAD_DIM].astype(BF16)
        vt_ref[0, kv, 0, HEAD_DIM:V_ROWS, :] = ones

    o += B_WIDTH + 2 * B_KV_WIDTH
    gate_ref[0] = jnp.dot(h, w_ref[:, o:o + D_MODEL], preferred_element_type=F32).astype(BF16)


def _proj0(x, g, w, cos, sin, qg, kg):
    bsz, s, _ = x.shape
    tm = TOKEN_TILE
    nck = s // tm
    tok = lambda width: pl.BlockSpec((1, tm, width), lambda b, i: (b, i, 0))
    const = lambda shape: pl.BlockSpec(shape, lambda b, i: (0,) * len(shape))
    return pl.pallas_call(
        _proj0_kernel,
        grid=(bsz, nck),
        in_specs=[tok(D_MODEL), const((1, D_MODEL)), const(w.shape),
                  pl.BlockSpec((tm, LANES), lambda b, i: (i, 0)),
                  pl.BlockSpec((tm, LANES), lambda b, i: (i, 0)),
                  const((1, LANES)), const((1, LANES))],
        out_specs=[tok(A_WIDTH), tok(A_WIDTH), tok(A_WIDTH),
                   pl.BlockSpec((1, B_HEADS, LANES, tm), lambda b, i: (b, 0, 0, i)),
                   pl.BlockSpec((1, B_KV_HEADS, tm, LANES), lambda b, i: (b, 0, i, 0)),
                   pl.BlockSpec((1, B_KV_HEADS, 1, V_ROWS, tm), lambda b, i: (b, 0, i, 0, 0)),
                   tok(D_MODEL)],
        out_shape=[jax.ShapeDtypeStruct((bsz, s, A_WIDTH), BF16)] * 3
        + [jax.ShapeDtypeStruct((bsz, B_HEADS, LANES, s), BF16),
           jax.ShapeDtypeStruct((bsz, B_KV_HEADS, s, LANES), BF16),
           jax.ShapeDtypeStruct((bsz, B_KV_HEADS, nck, V_ROWS, tm), BF16),
           jax.ShapeDtypeStruct((bsz, s, D_MODEL), BF16)],
        compiler_params=_params(2),
        name="proj0",
    )(x, g, w, cos, sin, qg, kg)


def _na_kernel(q_ref, k_ref, v_ref, bias_ref, o_ref, *, rows):
    t = pl.program_id(2)
    lane = lax.broadcasted_iota(jnp.int32, (GRID_W, LANES), 1)
    first = lane < HEAD_DIM
    win = NA_KH * GRID_W

    def body(g, carry):
        offs, scores = [], []
        for u in range(NA_GROUP):
            i = g * NA_GROUP + u
            r = t * NA_ROWS_PER_STEP + i
            rs = jnp.clip(r - NA_KH // 2, 0, rows - NA_KH)
            case = r - rs
            qoff = pl.multiple_of(i * GRID_W, GRID_W)
            koff = pl.multiple_of(rs * GRID_W, GRID_W)
            offs.append((qoff, koff))
            q = q_ref[0, pl.ds(qoff, GRID_W), :]
            kw = k_ref[0, pl.ds(koff, win), :]
            for hh in range(2):
                qm = jnp.where(first if hh == 0 else jnp.logical_not(first), q, jnp.zeros_like(q))
                scores.append(lax.dot_general(qm, kw, _NT, preferred_element_type=F32)
                              + bias_ref[hh, case])
        probs = []
        for sc in scores:
            e = jnp.exp2(sc - jnp.max(sc, axis=-1, keepdims=True))
            probs.append((e.astype(BF16), jnp.sum(e, axis=-1, keepdims=True)))
        for u, (qoff, koff) in enumerate(offs):
            vw = v_ref[0, pl.ds(koff, win), :]
            o0, o1 = [jnp.dot(e, vw, preferred_element_type=F32) / l for e, l in probs[2 * u:2 * u + 2]]
            o_ref[0, pl.ds(qoff, GRID_W), :] = jnp.where(first, o0, o1).astype(BF16)
        return carry

    lax.fori_loop(0, NA_ROWS_PER_STEP // NA_GROUP, body, 0)


def _na_bias_table(rpb):
    qc = np.arange(GRID_W)[:, None]
    kc = np.arange(GRID_W)[None, :]
    cs = np.clip(qc - NA_KW // 2, 0, GRID_W - NA_KW)
    valid = (kc >= cs) & (kc < cs + NA_KW)
    onehot = ((kc - qc + NA_KW - 1)[..., None] == np.arange(2 * NA_KW - 1)) & valid[..., None]
    cols = jnp.einsum("hrd,qkd->hrqk", rpb * LOG2E, jnp.asarray(onehot, F32),
                      precision=lax.Precision.HIGHEST)
    cols = jnp.where(valid[None, None], cols, MASKED)
    tbl = jnp.stack([cols[:, NA_KH - 1 - c:2 * NA_KH - 1 - c] for c in range(NA_KH)], axis=1)
    return tbl.transpose(0, 1, 3, 2, 4).reshape(rpb.shape[0], NA_KH, GRID_W, NA_KH * GRID_W)


def _na(qa, ka, va, bias):
    bsz, s, _ = qa.shape
    rows = s // GRID_W
    assert rows >= NA_KH and rows % NA_ROWS_PER_STEP == 0
    tq = NA_ROWS_PER_STEP * GRID_W
    return pl.pallas_call(
        functools.partial(_na_kernel, rows=rows),
        grid=(bsz, A_WIDTH // LANES, s // tq),
        in_specs=[pl.BlockSpec((1, tq, LANES), lambda b, hp, t: (b, t, hp)),
                  pl.BlockSpec((1, s, LANES), lambda b, hp, t: (b, 0, hp)),
                  pl.BlockSpec((1, s, LANES), lambda b, hp, t: (b, 0, hp)),
                  pl.BlockSpec((2, NA_KH, GRID_W, NA_KH * GRID_W), lambda b, hp, t: (hp, 0, 0, 0))],
        out_specs=pl.BlockSpec((1, tq, LANES), lambda b, hp, t: (b, t, hp)),
        out_shape=jax.ShapeDtypeStruct((bsz, s, A_WIDTH), BF16),
        compiler_params=_params(3),
        name="na",
    )(qa, ka, va, bias)


def _attend(streams, st_ref, p_ref):
    nck, _, tk = streams[0][2].shape
    n = streams[0][0].shape[1]

    def scores(i, j, par, h):
        q, k_ref, _ = streams[i]
        off = pl.multiple_of(j * tk + h * MXU_TILE, MXU_TILE)
        st = jnp.dot(k_ref[pl.ds(off, MXU_TILE), :], q, preferred_element_type=F32)
        st_ref[par][i, h * MXU_TILE:(h + 1) * MXU_TILE, :] = st
        c = st[0:SUBLANES]
        for r in range(SUBLANES, MXU_TILE, SUBLANES):
            c = jnp.maximum(c, st[r:r + SUBLANES])
        return c

    def softmax(i, par, m, cmax):
        m_new = jnp.maximum(m, jnp.max(cmax, axis=0, keepdims=True))
        for r in range(0, tk, SOFTMAX_SLAB):
            rows = slice(r, r + SOFTMAX_SLAB)
            p_ref[par][i, rows, :] = jnp.exp2(st_ref[par][i, rows, :] - m_new).astype(BF16)
        return m_new, jnp.exp2(m - m_new)

    def step(j, par, carry, with_scores=True, with_softmax=True):
        out = []
        for i, (m, alpha, acc, cmax) in enumerate(carry):
            acc = acc * alpha
            cmax_next = cmax
            for h in range(tk // MXU_TILE):
                if with_scores:
                    c = scores(i, j + 2, (par + 2) % RING, h)
                    cmax_next = c if h == 0 else jnp.maximum(cmax_next, c)
                rows = slice(h * MXU_TILE, (h + 1) * MXU_TILE)
                acc = acc + jnp.dot(streams[i][2][j, :, rows], p_ref[par][i, rows, :],
                                    preferred_element_type=F32)
            if with_softmax:
                m, alpha = softmax(i, (par + 1) % RING, m, cmax)
            out.append((m, alpha, acc, cmax_next))
        return tuple(out)

    def body(jj, carry):
        for u in range(STEPS_PER_TRIP):
            carry = step(STEPS_PER_TRIP * jj + u, u % RING, carry)
        return carry

    def chunk_scores(i, j, par):
        return functools.reduce(jnp.maximum, [scores(i, j, par, h) for h in range(tk // MXU_TILE)])

    carry = []
    for i in range(len(streams)):
        cmax0 = chunk_scores(i, 0, 0)
        cmax1 = chunk_scores(i, 1, 1) if nck > 1 else cmax0
        m, alpha = softmax(i, 0, jnp.full((1, n), -jnp.inf, F32), cmax0)
        carry.append((m, alpha, jnp.zeros((V_ROWS, n), F32), cmax1))
    carry = tuple(carry)
    trips = max(nck - 2, 0) // STEPS_PER_TRIP
    carry = lax.fori_loop(0, trips, body, carry)
    for j in range(STEPS_PER_TRIP * trips, nck):
        carry = step(j, j % RING, carry, with_scores=j + 2 < nck, with_softmax=j + 1 < nck)
    return [acc[0:HEAD_DIM] / acc[HEAD_DIM:HEAD_DIM + 1] for _, _, acc, _ in carry]


def _attend_scratch(n_streams, tk, n):
    return ([pltpu.VMEM((n_streams, tk, n), F32)] * RING + [pltpu.VMEM((n_streams, tk, n), BF16)] * RING)


def _tiles_per_step(nck, n_tiles):
    t = min(max(1, STEPS_PER_BLOCK // nck), n_tiles)
    while n_tiles % t:
        t -= 1
    return t


def _attend_tiles(tile_streams, scratch):
    sets = len(scratch) // (2 * RING)
    outs = []
    for t, streams in enumerate(tile_streams):
        base = (t % sets) * 2 * RING
        outs.append(_attend(streams, scratch[base:base + RING], scratch[base + RING:base + 2 * RING]))
    return outs


def _gqa_kernel(q_ref, k_ref, vt_ref, o_ref, *scratch):
    tq = scratch[0].shape[-1]
    tiles = q_ref.shape[3] // tq
    outs = _attend_tiles([[(q_ref[0, h, :, t * tq:(t + 1) * tq], k_ref.at[0, 0], vt_ref.at[0, 0])
                           for h in range(q_ref.shape[1])] for t in range(tiles)], scratch)
    for t, o in enumerate(outs):
        o_ref[0, t * tq:(t + 1) * tq, :] = jnp.concatenate(o, axis=0).T.astype(BF16)


def _gqa(qb, kb, vt):
    bsz, _, _, s = qb.shape
    tq = GQA_Q_TILE
    group = B_HEADS // B_KV_HEADS
    gw = group * HEAD_DIM
    tiles = _tiles_per_step(vt.shape[2], s // tq)
    return pl.pallas_call(
        _gqa_kernel,
        grid=(bsz, B_KV_HEADS, s // (tiles * tq)),
        in_specs=[pl.BlockSpec((1, group, LANES, tiles * tq), lambda b, kv, i: (b, kv, 0, i)),
                  pl.BlockSpec((1, 1, s, LANES), lambda b, kv, i: (b, kv, 0, 0)),
                  pl.BlockSpec((1, 1) + vt.shape[2:], lambda b, kv, i: (b, kv, 0, 0, 0))],
        out_specs=pl.BlockSpec((1, tiles * tq, gw), lambda b, kv, i: (b, i, kv)),
        out_shape=jax.ShapeDtypeStruct((bsz, s, B_WIDTH), BF16),
        scratch_shapes=_attend_scratch(group, vt.shape[-1], tq) * min(tiles, 2),
        compiler_params=_params(3),
        name="gqa",
    )(qb, kb, vt)


def _mla_kernel(q_ref, k_ref, vt_ref, o_ref, *scratch):
    sub = scratch[0].shape[-1]
    per_head = scratch[0].shape[0] // 2
    tq = per_head * sub
    tiles = q_ref.shape[3] // tq
    outs = _attend_tiles([[(q_ref[0, hh, :, t * tq + c * sub:t * tq + (c + 1) * sub],
                            k_ref.at[0, hh], vt_ref.at[0, hh])
                           for hh in range(2) for c in range(per_head)] for t in range(tiles)], scratch)
    for t, o in enumerate(outs):
        ot = jnp.concatenate([jnp.concatenate(o[hh * per_head:(hh + 1) * per_head], axis=1)
                              for hh in range(2)], axis=0)
        o_ref[0, t * tq:(t + 1) * tq, :] = ot.T.astype(BF16)


def _mla(q, k, vt):
    bsz, nh, _, s = q.shape
    tq = MLA_Q_TILE
    tiles = _tiles_per_step(vt.shape[2], s // tq)
    return pl.pallas_call(
        _mla_kernel,
        grid=(bsz, nh // 2, s // (tiles * tq)),
        in_specs=[pl.BlockSpec((1, 2, LANES, tiles * tq), lambda b, hp, i: (b, hp, 0, i)),
                  pl.BlockSpec((1, 2, s, LANES), lambda b, hp, i: (b, hp, 0, 0)),
                  pl.BlockSpec((1, 2) + vt.shape[2:], lambda b, hp, i: (b, hp, 0, 0, 0))],
        out_specs=pl.BlockSpec((1, tiles * tq, LANES), lambda b, hp, i: (b, i, hp)),
        out_shape=jax.ShapeDtypeStruct((bsz, s, nh * C_V), BF16),
        scratch_shapes=_attend_scratch(2 * tq // STREAM_QUERIES, vt.shape[-1], STREAM_QUERIES) * min(tiles, 2),
        compiler_params=_params(3),
        name="mla",
    )(q, k, vt)


def _out1_kernel(x_ref, m_ref, gate_ref, w_ref, g_ref, y_ref):
    gm = (m_ref[0].astype(F32) * _silu(gate_ref[0].astype(F32))).astype(BF16)
    y = x_ref[0] + jnp.dot(gm, w_ref[...], preferred_element_type=F32)
    y_ref[0] = _rms(y, g_ref[...])


def _out1(x, m, gate, w, g):
    bsz, s, _ = x.shape
    tm = TOKEN_TILE
    tok = pl.BlockSpec((1, tm, D_MODEL), lambda b, i: (b, i, 0))
    return pl.pallas_call(
        _out1_kernel,
        grid=(bsz, s // tm),
        in_specs=[tok, tok, tok, pl.BlockSpec(w.shape, lambda b, i: (0, 0)),
                  pl.BlockSpec((1, D_MODEL), lambda b, i: (0, 0))],
        out_specs=tok,
        out_shape=jax.ShapeDtypeStruct(x.shape, F32),
        compiler_params=_params(2),
        name="out1",
    )(x, m, gate, w, g)


def _mid_kernel(x_ref, ma_ref, mb_ref, gate0_ref, wout_ref, g_ref, w_ref, qg_ref, kvg_ref,
                wuqt_ref, wuk_ref, wuvt_ref, cos_ref, sin_ref, cost_ref, sint_ref,
                y_ref, q_ref, k_ref, vt_ref, gate_ref):
    tm = x_ref.shape[1]
    sg = _silu(gate0_ref[0].astype(F32))
    ga = (ma_ref[0].astype(F32) * sg[:, 0:A_WIDTH]).astype(BF16)
    gb = (mb_ref[0].astype(F32) * sg[:, A_WIDTH:]).astype(BF16)
    y = (x_ref[0] + jnp.dot(ga, wout_ref[0:A_WIDTH, :], preferred_element_type=F32)
         + jnp.dot(gb, wout_ref[A_WIDTH:, :], preferred_element_type=F32))
    y_ref[0] = y

    h = _rms(y, g_ref[...]).astype(BF16)
    lat_w = C_Q_RANK + C_KV_RANK + LANES
    lat = jnp.dot(h, w_ref[:, 0:lat_w], preferred_element_type=F32)
    gate_ref[0] = jnp.dot(h, w_ref[:, lat_w:], preferred_element_type=F32).astype(BF16)

    cqt = _rms(lat[:, 0:C_Q_RANK], qg_ref[...]).T.astype(BF16)
    qt_all = jnp.dot(wuqt_ref[...], cqt, preferred_element_type=F32)
    cost = cost_ref[...]
    sint = sint_ref[...]
    half = C_ROPE // 2
    for hd in range(C_HEADS):
        xq = qt_all[hd * LANES:(hd + 1) * LANES]
        partner = jnp.concatenate([xq[:C_NOPE], xq[C_NOPE + half:C_QK_DIM],
                                   xq[C_NOPE:C_NOPE + half], xq[C_QK_DIM:]], axis=0)
        q_ref[0, hd] = ((xq * cost + partner * sint) * (C_QK_DIM ** -0.5 * LOG2E)).astype(BF16)

    ckv = _rms(lat[:, C_Q_RANK:C_Q_RANK + C_KV_RANK], kvg_ref[...])
    kr = lat[:, C_Q_RANK + C_KV_RANK:lat_w]
    low = lax.broadcasted_iota(jnp.int32, (tm, LANES), 1) < C_NOPE + half
    partner = jnp.where(low, pltpu.roll(kr, LANES - half, 1), pltpu.roll(kr, half, 1))
    k_rope = kr * cos_ref[...] + partner * sin_ref[...]
    k_all = jnp.dot(ckv.astype(BF16), wuk_ref[...], preferred_element_type=F32)
    for hd in range(C_HEADS):
        k_ref[0, hd] = (k_all[:, hd * LANES:(hd + 1) * LANES] + k_rope).astype(BF16)

    vt_all = jnp.dot(wuvt_ref[...], ckv.T.astype(BF16), preferred_element_type=F32)
    ones = jnp.ones((V_ROWS - C_V, tm), BF16)
    for hd in range(C_HEADS):
        vt_ref[0, hd, 0, 0:C_V, :] = vt_all[hd * C_V:(hd + 1) * C_V].astype(BF16)
        vt_ref[0, hd, 0, C_V:V_ROWS, :] = ones


def _mid(x, ma, mb, gate0, wout, g, w, qg, kvg, wuqt, wuk, wuvt, cos, sin, cost, sint):
    bsz, s, _ = x.shape
    tm = TOKEN_TILE
    nck = s // tm
    tok = lambda width: pl.BlockSpec((1, tm, width), lambda b, i: (b, i, 0))
    const = lambda a: pl.BlockSpec(a.shape, lambda b, i: (0,) * a.ndim)
    heads = lambda shape, imap: pl.BlockSpec((1, C_HEADS) + shape, imap)
    return pl.pallas_call(
        _mid_kernel,
        grid=(bsz, nck),
        in_specs=[tok(D_MODEL), tok(A_WIDTH), tok(B_WIDTH), tok(D_MODEL), const(wout),
                  const(g), const(w), const(qg), const(kvg), const(wuqt), const(wuk), const(wuvt),
                  pl.BlockSpec((tm, LANES), lambda b, i: (i, 0)),
                  pl.BlockSpec((tm, LANES), lambda b, i: (i, 0)),
                  pl.BlockSpec((LANES, tm), lambda b, i: (0, i)),
                  pl.BlockSpec((LANES, tm), lambda b, i: (0, i))],
        out_specs=[tok(D_MODEL),
                   heads((LANES, tm), lambda b, i: (b, 0, 0, i)),
                   heads((tm, LANES), lambda b, i: (b, 0, i, 0)),
                   heads((1, V_ROWS, tm), lambda b, i: (b, 0, i, 0, 0)),
                   tok(D_MODEL)],
        out_shape=[jax.ShapeDtypeStruct(x.shape, F32),
                   jax.ShapeDtypeStruct((bsz, C_HEADS, LANES, s), BF16),
                   jax.ShapeDtypeStruct((bsz, C_HEADS, s, LANES), BF16),
                   jax.ShapeDtypeStruct((bsz, C_HEADS, nck, V_ROWS, tm), BF16),
                   jax.ShapeDtypeStruct((bsz, s, D_MODEL), BF16)],
        compiler_params=_params(2),
        name="mid",
    )(x, ma, mb, gate0, wout, g, w, qg, kvg, wuqt, wuk, wuvt, cos, sin, cost, sint)


def _axial_angles(n_tok, rot_dim):
    n_freq = rot_dim // 4
    inv = ROPE_THETA ** (-jnp.arange(n_freq, dtype=F32) / n_freq)
    t = jnp.arange(n_tok, dtype=jnp.int32)
    row = (t // GRID_W).astype(F32)
    col = (t % GRID_W).astype(F32)
    ang = jnp.concatenate([row[:, None] * inv[None], col[:, None] * inv[None]], axis=-1)
    return jnp.cos(ang), jnp.sin(ang)


def _rope_tables(s):
    c, sn = _axial_angles(s, HEAD_DIM)
    cos0 = jnp.tile(jnp.concatenate([c, c], axis=-1), (1, LANES // HEAD_DIM))
    sin0 = jnp.tile(jnp.concatenate([-sn, sn], axis=-1), (1, LANES // HEAD_DIM))
    c, sn = _axial_angles(s, C_ROPE)
    pad = LANES - C_QK_DIM
    cos1 = jnp.concatenate([jnp.ones((s, C_NOPE), F32), c, c, jnp.ones((s, pad), F32)], axis=-1)
    sin1 = jnp.concatenate([jnp.zeros((s, C_NOPE), F32), -sn, sn, jnp.zeros((s, pad), F32)], axis=-1)
    return cos0, sin0, cos1, sin1


def _prepare(norm_e, w_in_e, rpb_a, qnorm_b, knorm_b, w_out_e,
             norm_o, w_in_o, qlat_g, kvlat_g, w_uq, w_ukv, w_out_o, norm_f):
    rep = LANES // HEAD_DIM
    lat = C_Q_RANK + C_KV_RANK
    z = lambda n: jnp.zeros((D_MODEL, n), F32)
    w1 = jnp.concatenate([w_in_o[0][:, :lat], z(C_NOPE), w_in_o[0][:, lat:lat + C_ROPE],
                          z(LANES - C_QK_DIM), w_in_o[0][:, lat + C_ROPE:]], axis=1)
    wuq = jnp.pad(w_uq[0].reshape(C_Q_RANK, C_HEADS, C_QK_DIM),
                  ((0, 0), (0, 0), (0, LANES - C_QK_DIM))).reshape(C_Q_RANK, C_HEADS * LANES)
    wkv = w_ukv[0].reshape(C_KV_RANK, C_HEADS, C_NOPE + C_V)
    wuk = jnp.pad(wkv[:, :, :C_NOPE], ((0, 0), (0, 0), (0, LANES - C_NOPE))).reshape(C_KV_RANK, C_HEADS * LANES)
    wuv = wkv[:, :, C_NOPE:].reshape(C_KV_RANK, C_WIDTH)
    return dict(
        norm_e=norm_e[0][None], w_in_e=w_in_e[0].astype(BF16), bias=_na_bias_table(rpb_a[0]),
        qg=jnp.tile(qnorm_b[0], rep)[None], kg=jnp.tile(knorm_b[0], rep)[None],
        w_out_e=w_out_e[0].astype(BF16),
        norm_o=norm_o[0][None], w1=w1.astype(BF16), qlat_g=qlat_g[0][None], kvlat_g=kvlat_g[0][None],
        wuqt=wuq.T.astype(BF16), wuk=wuk.astype(BF16), wuvt=wuv.T.astype(BF16),
        w_out_o=w_out_o[0].astype(BF16), norm_f=norm_f[None])


def _trunk(x, p):
    cos0, sin0, cos1, sin1 = _rope_tables(x.shape[1])
    qa, ka, va, qb, kb, vbt, gate0 = _proj0(x, p["norm_e"], p["w_in_e"], cos0, sin0, p["qg"], p["kg"])
    mix_a = _na(qa, ka, va, p["bias"])
    mix_b = _gqa(qb, kb, vbt)
    x1, q, k, vt, gate1 = _mid(x, mix_a, mix_b, gate0, p["w_out_e"], p["norm_o"], p["w1"],
                               p["qlat_g"], p["kvlat_g"], p["wuqt"], p["wuk"], p["wuvt"],
                               cos1, sin1, cos1.T, sin1.T)
    mix_c = _mla(q, k, vt)
    return _out1(x1, mix_c, gate1, p["w_out_o"], p["norm_f"])


def kernel(x_prompt, x_sample, norm_e, w_in_e, rpb_a, qnorm_b, knorm_b, w_out_e,
           norm_o, w_in_o, qlat_g, kvlat_g, w_uq, w_ukv, w_out_o, norm_f):
    assert norm_e.shape[0] == 1 and norm_o.shape[0] == 1
    p = _prepare(norm_e, w_in_e, rpb_a, qnorm_b, knorm_b, w_out_e,
                 norm_o, w_in_o, qlat_g, kvlat_g, w_uq, w_ukv, w_out_o, norm_f)
    return (_trunk(x_prompt, p), _trunk(x_sample, p))
```

```python
import functools

import numpy as np
import jax
import jax.numpy as jnp
from jax import lax
from jax.experimental import pallas as pl
from jax.experimental.pallas import tpu as pltpu

D_MODEL = 1024
GRID_W = 64
HEAD_DIM = 64
A_HEADS = 8
NA_KH = 8
NA_KW = 16
B_HEADS = 8
B_KV_HEADS = 2
C_HEADS = 16
C_NOPE = 64
C_ROPE = 32
C_V = 64
C_Q_RANK = 384
C_KV_RANK = 256
ROPE_THETA = 10000.0
EPS = 1e-6

A_WIDTH = A_HEADS * HEAD_DIM
B_WIDTH = B_HEADS * HEAD_DIM
B_KV_WIDTH = B_KV_HEADS * HEAD_DIM
C_WIDTH = C_HEADS * C_V
C_QK_DIM = C_NOPE + C_ROPE

LANES = 128
SUBLANES = 8
MXU_TILE = 256
SOFTMAX_SLAB = 32
TOKEN_TILE = 512
V_ROWS = HEAD_DIM + 16
NA_ROWS_PER_STEP = 16
STREAM_QUERIES = 256
GQA_Q_TILE = STREAM_QUERIES
MLA_Q_TILE = 2 * STREAM_QUERIES
RING = 3
STEPS_PER_BLOCK = 32
MAX_TILES_PER_STEP = 4
STEPS_PER_TRIP = 30
assert STEPS_PER_TRIP % RING == 0
LOG2E = 1.4426950408889634
V7X_VMEM_BYTES = 64 * 1024 * 1024
VMEM_LIMIT = V7X_VMEM_BYTES * 7 // 8
MASKED = -1e30

F32 = jnp.float32
BF16 = jnp.bfloat16
_NT = (((1,), (1,)), ((), ()))


def _params(n_axes):
    return pltpu.CompilerParams(dimension_semantics=("arbitrary",) * n_axes,
                                vmem_limit_bytes=VMEM_LIMIT)


def _rms(x, g):
    return x * lax.rsqrt(jnp.mean(x * x, axis=-1, keepdims=True) + EPS) * g


def _silu(g):
    return g * jax.nn.sigmoid(g)


def _proj0_kernel(x_ref, g_ref, w_ref, cos_ref, sin_ref, qg_ref, kg_ref,
                  qa_ref, ka_ref, va_ref, qb_ref, kb_ref, vt_ref, gate_ref):
    tm = x_ref.shape[1]
    h = _rms(x_ref[0], g_ref[...]).astype(BF16)

    a = jnp.dot(h, w_ref[:, 0:3 * A_WIDTH], preferred_element_type=F32)
    qa_ref[0] = (a[:, 0:A_WIDTH] * (HEAD_DIM ** -0.5 * LOG2E)).astype(BF16)
    ka_ref[0] = a[:, A_WIDTH:2 * A_WIDTH].astype(BF16)
    va_ref[0] = a[:, 2 * A_WIDTH:3 * A_WIDTH].astype(BF16)

    o = 3 * A_WIDTH
    b = jnp.dot(h, w_ref[:, o:o + B_WIDTH + 2 * B_KV_WIDTH], preferred_element_type=F32)
    cos = cos_ref[...]
    sin = sin_ref[...]
    lane = lax.broadcasted_iota(jnp.int32, (tm, LANES), 1)
    first = lane < HEAD_DIM
    low = (lane % HEAD_DIM) < HEAD_DIM // 2

    def norm_rope(xc, g):
        sq = xc * xc
        sa = jnp.sum(jnp.where(first, sq, 0.0), axis=-1, keepdims=True)
        sb = jnp.sum(jnp.where(first, 0.0, sq), axis=-1, keepdims=True)
        ms = jnp.where(first, sa, sb) * (1.0 / HEAD_DIM)
        y = xc * lax.rsqrt(ms + EPS) * g
        partner = jnp.where(low, pltpu.roll(y, LANES - HEAD_DIM // 2, 1),
                            pltpu.roll(y, HEAD_DIM // 2, 1))
        return y * cos + partner * sin

    qg = qg_ref[...]
    for j in range(B_WIDTH // LANES):
        qj = norm_rope(b[:, j * LANES:(j + 1) * LANES], qg)
        qt = (qj * (HEAD_DIM ** -0.5 * LOG2E)).T.astype(BF16)
        upper = lax.broadcasted_iota(jnp.int32, qt.shape, 0) < HEAD_DIM
        qb_ref[0, 2 * j] = jnp.where(upper, qt, jnp.zeros_like(qt))
        qb_ref[0, 2 * j + 1] = jnp.where(upper, jnp.zeros_like(qt), qt)
    kk = norm_rope(b[:, B_WIDTH:B_WIDTH + LANES], kg_ref[...])
    ks = pltpu.roll(kk, HEAD_DIM, 1)
    kb_ref[0, 0] = jnp.where(first, kk, ks).astype(BF16)
    kb_ref[0, 1] = jnp.where(first, ks, kk).astype(BF16)
    vt = b[:, B_WIDTH + LANES:B_WIDTH + 2 * LANES].T
    ones = jnp.ones((V_ROWS - HEAD_DIM, tm), BF16)
    for kv in range(B_KV_HEADS):
        vt_ref[0, kv, 0, 0:HEAD_DIM, :] = vt[kv * HEAD_DIM:(kv + 1) * HEAD_DIM].astype(BF16)
        vt_ref[0, kv, 0, HEAD_DIM:V_ROWS, :] = ones

    o += B_WIDTH + 2 * B_KV_WIDTH
    gate_ref[0] = jnp.dot(h, w_ref[:, o:o + D_MODEL], preferred_element_type=F32).astype(BF16)


def _proj0(x, g, w, cos, sin, qg, kg):
    bsz, s, _ = x.shape
    tm = TOKEN_TILE
    nck = s // tm
    tok = lambda width: pl.BlockSpec((1, tm, width), lambda b, i: (b, i, 0))
    const = lambda shape: pl.BlockSpec(shape, lambda b, i: (0,) * len(shape))
    return pl.pallas_call(
        _proj0_kernel,
        grid=(bsz, nck),
        in_specs=[tok(D_MODEL), const((1, D_MODEL)), const(w.shape),
                  pl.BlockSpec((tm, LANES), lambda b, i: (i, 0)),
                  pl.BlockSpec((tm, LANES), lambda b, i: (i, 0)),
                  const((1, LANES)), const((1, LANES))],
        out_specs=[tok(A_WIDTH), tok(A_WIDTH), tok(A_WIDTH),
                   pl.BlockSpec((1, B_HEADS, LANES, tm), lambda b, i: (b, 0, 0, i)),
                   pl.BlockSpec((1, B_KV_HEADS, tm, LANES), lambda b, i: (b, 0, i, 0)),
                   pl.BlockSpec((1, B_KV_HEADS, 1, V_ROWS, tm), lambda b, i: (b, 0, i, 0, 0)),
                   tok(D_MODEL)],
        out_shape=[jax.ShapeDtypeStruct((bsz, s, A_WIDTH), BF16)] * 3
        + [jax.ShapeDtypeStruct((bsz, B_HEADS, LANES, s), BF16),
           jax.ShapeDtypeStruct((bsz, B_KV_HEADS, s, LANES), BF16),
           jax.ShapeDtypeStruct((bsz, B_KV_HEADS, nck, V_ROWS, tm), BF16),
           jax.ShapeDtypeStruct((bsz, s, D_MODEL), BF16)],
        compiler_params=_params(2),
        name="proj0",
    )(x, g, w, cos, sin, qg, kg)


def _na_kernel(q_ref, k_ref, v_ref, bias_ref, o_ref, *, rows):
    t = pl.program_id(2)
    rows_per_step = q_ref.shape[1] // GRID_W
    lane = lax.broadcasted_iota(jnp.int32, (GRID_W, LANES), 1)
    first = lane < HEAD_DIM
    win = NA_KH * GRID_W

    offs, scores = [], []
    for i in range(rows_per_step):
        r = t * rows_per_step + i
        rs = jnp.clip(r - NA_KH // 2, 0, rows - NA_KH)
        case = r - rs
        koff = pl.multiple_of(rs * GRID_W, GRID_W)
        offs.append(koff)
        q = q_ref[0, i * GRID_W:(i + 1) * GRID_W, :]
        zero = jnp.zeros_like(q)
        q2 = jnp.concatenate([jnp.where(first, q, zero), jnp.where(first, zero, q)], axis=0)
        bias = jnp.concatenate([bias_ref[0, case], bias_ref[1, case]], axis=0)
        scores.append(lax.dot_general(q2, k_ref[0, pl.ds(koff, win), :], _NT,
                                      preferred_element_type=F32) + bias)
    probs = []
    for sc in scores:
        e = jnp.exp2(sc - jnp.max(sc, axis=-1, keepdims=True))
        probs.append((e.astype(BF16), jnp.sum(e, axis=-1, keepdims=True)))
    for i, (koff, (e, l)) in enumerate(zip(offs, probs)):
        o2 = jnp.dot(e, v_ref[0, pl.ds(koff, win), :], preferred_element_type=F32) / l
        o_ref[0, i * GRID_W:(i + 1) * GRID_W, :] = jnp.where(first, o2[:GRID_W], o2[GRID_W:]).astype(BF16)


def _na_bias_table(rpb):
    qc = np.arange(GRID_W)[:, None]
    kc = np.arange(GRID_W)[None, :]
    cs = np.clip(qc - NA_KW // 2, 0, GRID_W - NA_KW)
    valid = (kc >= cs) & (kc < cs + NA_KW)
    onehot = ((kc - qc + NA_KW - 1)[..., None] == np.arange(2 * NA_KW - 1)) & valid[..., None]
    cols = jnp.einsum("hrd,qkd->hrqk", rpb * LOG2E, jnp.asarray(onehot, F32),
                      precision=lax.Precision.HIGHEST)
    cols = jnp.where(valid[None, None], cols, MASKED)
    tbl = jnp.stack([cols[:, NA_KH - 1 - c:2 * NA_KH - 1 - c] for c in range(NA_KH)], axis=1)
    return tbl.transpose(0, 1, 3, 2, 4).reshape(rpb.shape[0], NA_KH, GRID_W, NA_KH * GRID_W)


def _na(qa, ka, va, bias):
    bsz, s, _ = qa.shape
    rows = s // GRID_W
    rows_per_step = min(NA_ROWS_PER_STEP, rows)
    assert rows >= NA_KH and rows % rows_per_step == 0
    tq = rows_per_step * GRID_W
    return pl.pallas_call(
        functools.partial(_na_kernel, rows=rows),
        grid=(bsz, A_WIDTH // LANES, s // tq),
        in_specs=[pl.BlockSpec((1, tq, LANES), lambda b, hp, t: (b, t, hp)),
                  pl.BlockSpec((1, s, LANES), lambda b, hp, t: (b, 0, hp)),
                  pl.BlockSpec((1, s, LANES), lambda b, hp, t: (b, 0, hp)),
                  pl.BlockSpec((2, NA_KH, GRID_W, NA_KH * GRID_W), lambda b, hp, t: (hp, 0, 0, 0))],
        out_specs=pl.BlockSpec((1, tq, LANES), lambda b, hp, t: (b, t, hp)),
        out_shape=jax.ShapeDtypeStruct((bsz, s, A_WIDTH), BF16),
        compiler_params=_params(3),
        name="na",
    )(qa, ka, va, bias)


def _attend(streams, st_ref, p_ref):
    nck, _, tk = streams[0][2].shape
    n = streams[0][0].shape[1]

    def scores(i, j, par, h):
        q, k_ref, _ = streams[i]
        off = pl.multiple_of(j * tk + h * MXU_TILE, MXU_TILE)
        st = jnp.dot(k_ref[pl.ds(off, MXU_TILE), :], q, preferred_element_type=F32)
        st_ref[par][i, h * MXU_TILE:(h + 1) * MXU_TILE, :] = st
        c = st[0:SUBLANES]
        for r in range(SUBLANES, MXU_TILE, SUBLANES):
            c = jnp.maximum(c, st[r:r + SUBLANES])
        return c

    def softmax(i, par, m, cmax):
        m_new = jnp.maximum(m, jnp.max(cmax, axis=0, keepdims=True))
        for r in range(0, tk, SOFTMAX_SLAB):
            rows = slice(r, r + SOFTMAX_SLAB)
            p_ref[par][i, rows, :] = jnp.exp2(st_ref[par][i, rows, :] - m_new).astype(BF16)
        return m_new, jnp.exp2(m - m_new)

    def step(j, par, carry, with_scores=True, with_softmax=True):
        out = []
        for i, (m, alpha, acc, cmax) in enumerate(carry):
            acc = acc * alpha
            cmax_next = cmax
            for h in range(tk // MXU_TILE):
                if with_scores:
                    c = scores(i, j + 2, (par + 2) % RING, h)
                    cmax_next = c if h == 0 else jnp.maximum(cmax_next, c)
                rows = slice(h * MXU_TILE, (h + 1) * MXU_TILE)
                acc = acc + jnp.dot(streams[i][2][j, :, rows], p_ref[par][i, rows, :],
                                    preferred_element_type=F32)
            if with_softmax:
                m, alpha = softmax(i, (par + 1) % RING, m, cmax)
            out.append((m, alpha, acc, cmax_next))
        return tuple(out)

    def body(jj, carry):
        for u in range(STEPS_PER_TRIP):
            carry = step(STEPS_PER_TRIP * jj + u, u % RING, carry)
        return carry

    def chunk_scores(i, j, par):
        return functools.reduce(jnp.maximum, [scores(i, j, par, h) for h in range(tk // MXU_TILE)])

    carry = []
    for i in range(len(streams)):
        cmax0 = chunk_scores(i, 0, 0)
        cmax1 = chunk_scores(i, 1, 1) if nck > 1 else cmax0
        m, alpha = softmax(i, 0, jnp.full((1, n), -jnp.inf, F32), cmax0)
        carry.append((m, alpha, jnp.zeros((V_ROWS, n), F32), cmax1))
    carry = tuple(carry)
    trips = max(nck - 2, 0) // STEPS_PER_TRIP
    carry = lax.fori_loop(0, trips, body, carry)
    for j in range(STEPS_PER_TRIP * trips, nck):
        carry = step(j, j % RING, carry, with_scores=j + 2 < nck, with_softmax=j + 1 < nck)
    return [acc[0:HEAD_DIM] / acc[HEAD_DIM:HEAD_DIM + 1] for _, _, acc, _ in carry]


def _attend_scratch(n_streams, tk, n):
    return ([pltpu.VMEM((n_streams, tk, n), F32)] * RING + [pltpu.VMEM((n_streams, tk, n), BF16)] * RING)


def _tiles_per_step(nck, n_tiles):
    t = min(max(1, STEPS_PER_BLOCK // nck), MAX_TILES_PER_STEP, n_tiles)
    while n_tiles % t:
        t -= 1
    return t


def _attend_tiles(tile_streams, scratch):
    sets = len(scratch) // (2 * RING)
    outs = []
    for t, streams in enumerate(tile_streams):
        base = (t % sets) * 2 * RING
        outs.append(_attend(streams, scratch[base:base + RING], scratch[base + RING:base + 2 * RING]))
    return outs


def _gqa_kernel(q_ref, k_ref, vt_ref, o_ref, *scratch):
    tq = scratch[0].shape[-1]
    tiles = q_ref.shape[3] // tq
    outs = _attend_tiles([[(q_ref[0, h, :, t * tq:(t + 1) * tq], k_ref.at[0, 0], vt_ref.at[0, 0])
                           for h in range(q_ref.shape[1])] for t in range(tiles)], scratch)
    for t, o in enumerate(outs):
        o_ref[0, t * tq:(t + 1) * tq, :] = jnp.concatenate(o, axis=0).T.astype(BF16)


def _gqa(qb, kb, vt):
    bsz, _, _, s = qb.shape
    tq = GQA_Q_TILE
    group = B_HEADS // B_KV_HEADS
    gw = group * HEAD_DIM
    tiles = _tiles_per_step(vt.shape[2], s // tq)
    return pl.pallas_call(
        _gqa_kernel,
        grid=(bsz, B_KV_HEADS, s // (tiles * tq)),
        in_specs=[pl.BlockSpec((1, group, LANES, tiles * tq), lambda b, kv, i: (b, kv, 0, i)),
                  pl.BlockSpec((1, 1, s, LANES), lambda b, kv, i: (b, kv, 0, 0)),
                  pl.BlockSpec((1, 1) + vt.shape[2:], lambda b, kv, i: (b, kv, 0, 0, 0))],
        out_specs=pl.BlockSpec((1, tiles * tq, gw), lambda b, kv, i: (b, i, kv)),
        out_shape=jax.ShapeDtypeStruct((bsz, s, B_WIDTH), BF16),
        scratch_shapes=_attend_scratch(group, vt.shape[-1], tq) * min(tiles, 2),
        compiler_params=_params(3),
        name="gqa",
    )(qb, kb, vt)


def _mla_kernel(q_ref, k_ref, vt_ref, o_ref, *scratch):
    sub = scratch[0].shape[-1]
    per_head = scratch[0].shape[0] // 2
    tq = per_head * sub
    tiles = q_ref.shape[3] // tq
    outs = _attend_tiles([[(q_ref[0, hh, :, t * tq + c * sub:t * tq + (c + 1) * sub],
                            k_ref.at[0, hh], vt_ref.at[0, hh])
                           for hh in range(2) for c in range(per_head)] for t in range(tiles)], scratch)
    for t, o in enumerate(outs):
        ot = jnp.concatenate([jnp.concatenate(o[hh * per_head:(hh + 1) * per_head], axis=1)
                              for hh in range(2)], axis=0)
        o_ref[0, t * tq:(t + 1) * tq, :] = ot.T.astype(BF16)


def _mla(q, k, vt):
    bsz, nh, _, s = q.shape
    tq = MLA_Q_TILE
    tiles = _tiles_per_step(vt.shape[2], s // tq)
    return pl.pallas_call(
        _mla_kernel,
        grid=(bsz, nh // 2, s // (tiles * tq)),
        in_specs=[pl.BlockSpec((1, 2, LANES, tiles * tq), lambda b, hp, i: (b, hp, 0, i)),
                  pl.BlockSpec((1, 2, s, LANES), lambda b, hp, i: (b, hp, 0, 0)),
                  pl.BlockSpec((1, 2) + vt.shape[2:], lambda b, hp, i: (b, hp, 0, 0, 0))],
        out_specs=pl.BlockSpec((1, tiles * tq, LANES), lambda b, hp, i: (b, i, hp)),
        out_shape=jax.ShapeDtypeStruct((bsz, s, nh * C_V), BF16),
        scratch_shapes=_attend_scratch(2 * tq // STREAM_QUERIES, vt.shape[-1], STREAM_QUERIES) * min(tiles, 2),
        compiler_params=_params(3),
        name="mla",
    )(q, k, vt)


def _out1_kernel(x_ref, m_ref, gate_ref, w_ref, g_ref, y_ref):
    gm = (m_ref[0].astype(F32) * _silu(gate_ref[0].astype(F32))).astype(BF16)
    y = x_ref[0] + jnp.dot(gm, w_ref[...], preferred_element_type=F32)
    y_ref[0] = _rms(y, g_ref[...])


def _out1(x, m, gate, w, g):
    bsz, s, _ = x.shape
    tm = TOKEN_TILE
    tok = pl.BlockSpec((1, tm, D_MODEL), lambda b, i: (b, i, 0))
    return pl.pallas_call(
        _out1_kernel,
        grid=(bsz, s // tm),
        in_specs=[tok, tok, tok, pl.BlockSpec(w.shape, lambda b, i: (0, 0)),
                  pl.BlockSpec((1, D_MODEL), lambda b, i: (0, 0))],
        out_specs=tok,
        out_shape=jax.ShapeDtypeStruct(x.shape, F32),
        compiler_params=_params(2),
        name="out1",
    )(x, m, gate, w, g)


def _mid_kernel(x_ref, ma_ref, mb_ref, gate0_ref, wout_ref, g_ref, w_ref, qg_ref, kvg_ref,
                wuqt_ref, wuk_ref, wuvt_ref, cos_ref, sin_ref, cost_ref, sint_ref,
                y_ref, q_ref, k_ref, vt_ref, gate_ref):
    tm = x_ref.shape[1]
    sg = _silu(gate0_ref[0].astype(F32))
    ga = (ma_ref[0].astype(F32) * sg[:, 0:A_WIDTH]).astype(BF16)
    gb = (mb_ref[0].astype(F32) * sg[:, A_WIDTH:]).astype(BF16)
    y = (x_ref[0] + jnp.dot(ga, wout_ref[0:A_WIDTH, :], preferred_element_type=F32)
         + jnp.dot(gb, wout_ref[A_WIDTH:, :], preferred_element_type=F32))
    y_ref[0] = y

    h = _rms(y, g_ref[...]).astype(BF16)
    lat_w = C_Q_RANK + C_KV_RANK + LANES
    lat = jnp.dot(h, w_ref[:, 0:lat_w], preferred_element_type=F32)
    gate_ref[0] = jnp.dot(h, w_ref[:, lat_w:], preferred_element_type=F32).astype(BF16)

    cqt = _rms(lat[:, 0:C_Q_RANK], qg_ref[...]).T.astype(BF16)
    qt_all = jnp.dot(wuqt_ref[...], cqt, preferred_element_type=F32)
    cost = cost_ref[...]
    sint = sint_ref[...]
    half = C_ROPE // 2
    for hd in range(C_HEADS):
        xq = qt_all[hd * LANES:(hd + 1) * LANES]
        partner = jnp.concatenate([xq[:C_NOPE], xq[C_NOPE + half:C_QK_DIM],
                                   xq[C_NOPE:C_NOPE + half], xq[C_QK_DIM:]], axis=0)
        q_ref[0, hd] = ((xq * cost + partner * sint) * (C_QK_DIM ** -0.5 * LOG2E)).astype(BF16)

    ckv = _rms(lat[:, C_Q_RANK:C_Q_RANK + C_KV_RANK], kvg_ref[...])
    kr = lat[:, C_Q_RANK + C_KV_RANK:lat_w]
    low = lax.broadcasted_iota(jnp.int32, (tm, LANES), 1) < C_NOPE + half
    partner = jnp.where(low, pltpu.roll(kr, LANES - half, 1), pltpu.roll(kr, half, 1))
    k_rope = kr * cos_ref[...] + partner * sin_ref[...]
    k_all = jnp.dot(ckv.astype(BF16), wuk_ref[...], preferred_element_type=F32)
    for hd in range(C_HEADS):
        k_ref[0, hd] = (k_all[:, hd * LANES:(hd + 1) * LANES] + k_rope).astype(BF16)

    vt_all = jnp.dot(wuvt_ref[...], ckv.T.astype(BF16), preferred_element_type=F32)
    ones = jnp.ones((V_ROWS - C_V, tm), BF16)
    for hd in range(C_HEADS):
        vt_ref[0, hd, 0, 0:C_V, :] = vt_all[hd * C_V:(hd + 1) * C_V].astype(BF16)
        vt_ref[0, hd, 0, C_V:V_ROWS, :] = ones


def _mid(x, ma, mb, gate0, wout, g, w, qg, kvg, wuqt, wuk, wuvt, cos, sin, cost, sint):
    bsz, s, _ = x.shape
    tm = TOKEN_TILE
    nck = s // tm
    tok = lambda width: pl.BlockSpec((1, tm, width), lambda b, i: (b, i, 0))
    const = lambda a: pl.BlockSpec(a.shape, lambda b, i: (0,) * a.ndim)
    heads = lambda shape, imap: pl.BlockSpec((1, C_HEADS) + shape, imap)
    return pl.pallas_call(
        _mid_kernel,
        grid=(bsz, nck),
        in_specs=[tok(D_MODEL), tok(A_WIDTH), tok(B_WIDTH), tok(D_MODEL), const(wout),
                  const(g), const(w), const(qg), const(kvg), const(wuqt), const(wuk), const(wuvt),
                  pl.BlockSpec((tm, LANES), lambda b, i: (i, 0)),
                  pl.BlockSpec((tm, LANES), lambda b, i: (i, 0)),
                  pl.BlockSpec((LANES, tm), lambda b, i: (0, i)),
                  pl.BlockSpec((LANES, tm), lambda b, i: (0, i))],
        out_specs=[tok(D_MODEL),
                   heads((LANES, tm), lambda b, i: (b, 0, 0, i)),
                   heads((tm, LANES), lambda b, i: (b, 0, i, 0)),
                   heads((1, V_ROWS, tm), lambda b, i: (b, 0, i, 0, 0)),
                   tok(D_MODEL)],
        out_shape=[jax.ShapeDtypeStruct(x.shape, F32),
                   jax.ShapeDtypeStruct((bsz, C_HEADS, LANES, s), BF16),
                   jax.ShapeDtypeStruct((bsz, C_HEADS, s, LANES), BF16),
                   jax.ShapeDtypeStruct((bsz, C_HEADS, nck, V_ROWS, tm), BF16),
                   jax.ShapeDtypeStruct((bsz, s, D_MODEL), BF16)],
        compiler_params=_params(2),
        name="mid",
    )(x, ma, mb, gate0, wout, g, w, qg, kvg, wuqt, wuk, wuvt, cos, sin, cost, sint)


def _axial_angles(n_tok, rot_dim):
    n_freq = rot_dim // 4
    inv = ROPE_THETA ** (-jnp.arange(n_freq, dtype=F32) / n_freq)
    t = jnp.arange(n_tok, dtype=jnp.int32)
    row = (t // GRID_W).astype(F32)
    col = (t % GRID_W).astype(F32)
    ang = jnp.concatenate([row[:, None] * inv[None], col[:, None] * inv[None]], axis=-1)
    return jnp.cos(ang), jnp.sin(ang)


def _rope_tables(s):
    c, sn = _axial_angles(s, HEAD_DIM)
    cos0 = jnp.tile(jnp.concatenate([c, c], axis=-1), (1, LANES // HEAD_DIM))
    sin0 = jnp.tile(jnp.concatenate([-sn, sn], axis=-1), (1, LANES // HEAD_DIM))
    c, sn = _axial_angles(s, C_ROPE)
    pad = LANES - C_QK_DIM
    cos1 = jnp.concatenate([jnp.ones((s, C_NOPE), F32), c, c, jnp.ones((s, pad), F32)], axis=-1)
    sin1 = jnp.concatenate([jnp.zeros((s, C_NOPE), F32), -sn, sn, jnp.zeros((s, pad), F32)], axis=-1)
    return cos0, sin0, cos1, sin1


def _prepare(norm_e, w_in_e, rpb_a, qnorm_b, knorm_b, w_out_e,
             norm_o, w_in_o, qlat_g, kvlat_g, w_uq, w_ukv, w_out_o, norm_f):
    rep = LANES // HEAD_DIM
    lat = C_Q_RANK + C_KV_RANK
    z = lambda n: jnp.zeros((D_MODEL, n), F32)
    w1 = jnp.concatenate([w_in_o[0][:, :lat], z(C_NOPE), w_in_o[0][:, lat:lat + C_ROPE],
                          z(LANES - C_QK_DIM), w_in_o[0][:, lat + C_ROPE:]], axis=1)
    wuq = jnp.pad(w_uq[0].reshape(C_Q_RANK, C_HEADS, C_QK_DIM),
                  ((0, 0), (0, 0), (0, LANES - C_QK_DIM))).reshape(C_Q_RANK, C_HEADS * LANES)
    wkv = w_ukv[0].reshape(C_KV_RANK, C_HEADS, C_NOPE + C_V)
    wuk = jnp.pad(wkv[:, :, :C_NOPE], ((0, 0), (0, 0), (0, LANES - C_NOPE))).reshape(C_KV_RANK, C_HEADS * LANES)
    wuv = wkv[:, :, C_NOPE:].reshape(C_KV_RANK, C_WIDTH)
    return dict(
        norm_e=norm_e[0][None], w_in_e=w_in_e[0].astype(BF16), bias=_na_bias_table(rpb_a[0]),
        qg=jnp.tile(qnorm_b[0], rep)[None], kg=jnp.tile(knorm_b[0], rep)[None],
        w_out_e=w_out_e[0].astype(BF16),
        norm_o=norm_o[0][None], w1=w1.astype(BF16), qlat_g=qlat_g[0][None], kvlat_g=kvlat_g[0][None],
        wuqt=wuq.T.astype(BF16), wuk=wuk.astype(BF16), wuvt=wuv.T.astype(BF16),
        w_out_o=w_out_o[0].astype(BF16), norm_f=norm_f[None])


def _trunk(x, p):
    cos0, sin0, cos1, sin1 = _rope_tables(x.shape[1])
    qa, ka, va, qb, kb, vbt, gate0 = _proj0(x, p["norm_e"], p["w_in_e"], cos0, sin0, p["qg"], p["kg"])
    mix_a = _na(qa, ka, va, p["bias"])
    mix_b = _gqa(qb, kb, vbt)
    x1, q, k, vt, gate1 = _mid(x, mix_a, mix_b, gate0, p["w_out_e"], p["norm_o"], p["w1"],
                               p["qlat_g"], p["kvlat_g"], p["wuqt"], p["wuk"], p["wuvt"],
                               cos1, sin1, cos1.T, sin1.T)
    mix_c = _mla(q, k, vt)
    return _out1(x1, mix_c, gate1, p["w_out_o"], p["norm_f"])


def kernel(x_prompt, x_sample, norm_e, w_in_e, rpb_a, qnorm_b, knorm_b, w_out_e,
           norm_o, w_in_o, qlat_g, kvlat_g, w_uq, w_ukv, w_out_o, norm_f):
    assert norm_e.shape[0] == 1 and norm_o.shape[0] == 1
    p = _prepare(norm_e, w_in_e, rpb_a, qnorm_b, knorm_b, w_out_e,
                 norm_o, w_in_o, qlat_g, kvlat_g, w_uq, w_ukv, w_out_o, norm_f)
    return (_trunk(x_prompt, p), _trunk(x_sample, p))
```

```python
import functools

import numpy as np
import jax
import jax.numpy as jnp
from jax import lax
from jax.experimental import pallas as pl
from jax.experimental.pallas import tpu as pltpu

D_MODEL = 1024
GRID_W = 64
HEAD_DIM = 64
A_HEADS = 8
NA_KH = 8
NA_KW = 16
B_HEADS = 8
B_KV_HEADS = 2
C_HEADS = 16
C_NOPE = 64
C_ROPE = 32
C_V = 64
C_Q_RANK = 384
C_KV_RANK = 256
ROPE_THETA = 10000.0
EPS = 1e-6

A_WIDTH = A_HEADS * HEAD_DIM
B_WIDTH = B_HEADS * HEAD_DIM
B_KV_WIDTH = B_KV_HEADS * HEAD_DIM
C_WIDTH = C_HEADS * C_V
C_QK_DIM = C_NOPE + C_ROPE

LANES = 128
SUBLANES = 8
MXU_TILE = 256
SOFTMAX_SLAB = 32
TOKEN_TILE = 512
V_ROWS = HEAD_DIM + 16
NA_ROWS_PER_STEP = 32
STREAM_QUERIES = 256
GQA_Q_TILE = STREAM_QUERIES
MLA_Q_TILE = 2 * STREAM_QUERIES
RING = 3
STEPS_PER_BLOCK = 32
MAX_TILES_PER_STEP = 4
STEPS_PER_TRIP = 30
assert STEPS_PER_TRIP % RING == 0
LOG2E = 1.4426950408889634
V7X_VMEM_BYTES = 64 * 1024 * 1024
VMEM_LIMIT = V7X_VMEM_BYTES * 7 // 8
MASKED = -1e30

F32 = jnp.float32
BF16 = jnp.bfloat16
_NT = (((1,), (1,)), ((), ()))


def _params(n_axes):
    return pltpu.CompilerParams(dimension_semantics=("arbitrary",) * n_axes,
                                vmem_limit_bytes=VMEM_LIMIT)


def _rms(x, g):
    return x * lax.rsqrt(jnp.mean(x * x, axis=-1, keepdims=True) + EPS) * g


def _silu(g):
    return g * jax.nn.sigmoid(g)


def _proj0_kernel(x_ref, g_ref, w_ref, cos_ref, sin_ref, qg_ref, kg_ref,
                  qa_ref, ka_ref, va_ref, qb_ref, kb_ref, vt_ref, gate_ref):
    tm = x_ref.shape[1]
    h = _rms(x_ref[0], g_ref[...]).astype(BF16)

    a = jnp.dot(h, w_ref[:, 0:3 * A_WIDTH], preferred_element_type=F32)
    qa_ref[0] = (a[:, 0:A_WIDTH] * (HEAD_DIM ** -0.5 * LOG2E)).astype(BF16)
    ka_ref[0] = a[:, A_WIDTH:2 * A_WIDTH].astype(BF16)
    va_ref[0] = a[:, 2 * A_WIDTH:3 * A_WIDTH].astype(BF16)

    o = 3 * A_WIDTH
    b = jnp.dot(h, w_ref[:, o:o + B_WIDTH + 2 * B_KV_WIDTH], preferred_element_type=F32)
    cos = cos_ref[...]
    sin = sin_ref[...]
    lane = lax.broadcasted_iota(jnp.int32, (tm, LANES), 1)
    first = lane < HEAD_DIM
    low = (lane % HEAD_DIM) < HEAD_DIM // 2

    def norm_rope(xc, g):
        sq = xc * xc
        sa = jnp.sum(jnp.where(first, sq, 0.0), axis=-1, keepdims=True)
        sb = jnp.sum(jnp.where(first, 0.0, sq), axis=-1, keepdims=True)
        ms = jnp.where(first, sa, sb) * (1.0 / HEAD_DIM)
        y = xc * lax.rsqrt(ms + EPS) * g
        partner = jnp.where(low, pltpu.roll(y, LANES - HEAD_DIM // 2, 1),
                            pltpu.roll(y, HEAD_DIM // 2, 1))
        return y * cos + partner * sin

    qg = qg_ref[...]
    for j in range(B_WIDTH // LANES):
        qj = norm_rope(b[:, j * LANES:(j + 1) * LANES], qg)
        qt = (qj * (HEAD_DIM ** -0.5 * LOG2E)).T.astype(BF16)
        upper = lax.broadcasted_iota(jnp.int32, qt.shape, 0) < HEAD_DIM
        qb_ref[0, 2 * j] = jnp.where(upper, qt, jnp.zeros_like(qt))
        qb_ref[0, 2 * j + 1] = jnp.where(upper, jnp.zeros_like(qt), qt)
    kk = norm_rope(b[:, B_WIDTH:B_WIDTH + LANES], kg_ref[...])
    ks = pltpu.roll(kk, HEAD_DIM, 1)
    kb_ref[0, 0] = jnp.where(first, kk, ks).astype(BF16)
    kb_ref[0, 1] = jnp.where(first, ks, kk).astype(BF16)
    vt = b[:, B_WIDTH + LANES:B_WIDTH + 2 * LANES].T
    ones = jnp.ones((V_ROWS - HEAD_DIM, tm), BF16)
    for kv in range(B_KV_HEADS):
        vt_ref[0, kv, 0, 0:HEAD_DIM, :] = vt[kv * HEAD_DIM:(kv + 1) * HEAD_DIM].astype(BF16)
        vt_ref[0, kv, 0, HEAD_DIM:V_ROWS, :] = ones

    o += B_WIDTH + 2 * B_KV_WIDTH
    gate_ref[0] = jnp.dot(h, w_ref[:, o:o + D_MODEL], preferred_element_type=F32).astype(BF16)


def _proj0(x, g, w, cos, sin, qg, kg):
    bsz, s, _ = x.shape
    tm = TOKEN_TILE
    nck = s // tm
    tok = lambda width: pl.BlockSpec((1, tm, width), lambda b, i: (b, i, 0))
    const = lambda shape: pl.BlockSpec(shape, lambda b, i: (0,) * len(shape))
    return pl.pallas_call(
        _proj0_kernel,
        grid=(bsz, nck),
        in_specs=[tok(D_MODEL), const((1, D_MODEL)), const(w.shape),
                  pl.BlockSpec((tm, LANES), lambda b, i: (i, 0)),
                  pl.BlockSpec((tm, LANES), lambda b, i: (i, 0)),
                  const((1, LANES)), const((1, LANES))],
        out_specs=[tok(A_WIDTH), tok(A_WIDTH), tok(A_WIDTH),
                   pl.BlockSpec((1, B_HEADS, LANES, tm), lambda b, i: (b, 0, 0, i)),
                   pl.BlockSpec((1, B_KV_HEADS, tm, LANES), lambda b, i: (b, 0, i, 0)),
                   pl.BlockSpec((1, B_KV_HEADS, 1, V_ROWS, tm), lambda b, i: (b, 0, i, 0, 0)),
                   tok(D_MODEL)],
        out_shape=[jax.ShapeDtypeStruct((bsz, s, A_WIDTH), BF16)] * 3
        + [jax.ShapeDtypeStruct((bsz, B_HEADS, LANES, s), BF16),
           jax.ShapeDtypeStruct((bsz, B_KV_HEADS, s, LANES), BF16),
           jax.ShapeDtypeStruct((bsz, B_KV_HEADS, nck, V_ROWS, tm), BF16),
           jax.ShapeDtypeStruct((bsz, s, D_MODEL), BF16)],
        compiler_params=_params(2),
        name="proj0",
    )(x, g, w, cos, sin, qg, kg)


def _na_kernel(q_ref, k_ref, v_ref, bias_ref, o_ref, *, rows):
    t = pl.program_id(2)
    rows_per_step = q_ref.shape[1] // GRID_W
    lane = lax.broadcasted_iota(jnp.int32, (GRID_W, LANES), 1)
    first = lane < HEAD_DIM
    win = NA_KH * GRID_W

    def scores(i):
        r = t * rows_per_step + i
        rs = jnp.clip(r - NA_KH // 2, 0, rows - NA_KH)
        case = r - rs
        koff = pl.multiple_of(rs * GRID_W, GRID_W)
        q = q_ref[0, i * GRID_W:(i + 1) * GRID_W, :]
        zero = jnp.zeros_like(q)
        q2 = jnp.concatenate([jnp.where(first, q, zero), jnp.where(first, zero, q)], axis=0)
        bias = jnp.concatenate([bias_ref[0, case], bias_ref[1, case]], axis=0)
        sc = lax.dot_general(q2, k_ref[0, pl.ds(koff, win), :], _NT, preferred_element_type=F32) + bias
        return koff, sc

    def softmax(koff, sc):
        e = jnp.exp2(sc - jnp.max(sc, axis=-1, keepdims=True))
        return koff, e.astype(BF16), jnp.sum(e, axis=-1, keepdims=True)

    def attend(i, koff, e, l):
        o2 = jnp.dot(e, v_ref[0, pl.ds(koff, win), :], preferred_element_type=F32) / l
        o_ref[0, i * GRID_W:(i + 1) * GRID_W, :] = jnp.where(first, o2[:GRID_W], o2[GRID_W:]).astype(BF16)

    scored = probs = None
    for i in range(rows_per_step + 2):
        nxt = scores(i) if i < rows_per_step else None
        if probs is not None:
            attend(i - 2, *probs)
        probs = softmax(*scored) if scored is not None else None
        scored = nxt


def _na_bias_table(rpb):
    qc = np.arange(GRID_W)[:, None]
    kc = np.arange(GRID_W)[None, :]
    cs = np.clip(qc - NA_KW // 2, 0, GRID_W - NA_KW)
    valid = (kc >= cs) & (kc < cs + NA_KW)
    onehot = ((kc - qc + NA_KW - 1)[..., None] == np.arange(2 * NA_KW - 1)) & valid[..., None]
    cols = jnp.einsum("hrd,qkd->hrqk", rpb * LOG2E, jnp.asarray(onehot, F32),
                      precision=lax.Precision.HIGHEST)
    cols = jnp.where(valid[None, None], cols, MASKED)
    tbl = jnp.stack([cols[:, NA_KH - 1 - c:2 * NA_KH - 1 - c] for c in range(NA_KH)], axis=1)
    return tbl.transpose(0, 1, 3, 2, 4).reshape(rpb.shape[0], NA_KH, GRID_W, NA_KH * GRID_W)


def _na(qa, ka, va, bias):
    bsz, s, _ = qa.shape
    rows = s // GRID_W
    rows_per_step = min(NA_ROWS_PER_STEP, rows)
    assert rows >= NA_KH and rows % rows_per_step == 0
    tq = rows_per_step * GRID_W
    return pl.pallas_call(
        functools.partial(_na_kernel, rows=rows),
        grid=(bsz, A_WIDTH // LANES, s // tq),
        in_specs=[pl.BlockSpec((1, tq, LANES), lambda b, hp, t: (b, t, hp)),
                  pl.BlockSpec((1, s, LANES), lambda b, hp, t: (b, 0, hp)),
                  pl.BlockSpec((1, s, LANES), lambda b, hp, t: (b, 0, hp)),
                  pl.BlockSpec((2, NA_KH, GRID_W, NA_KH * GRID_W), lambda b, hp, t: (hp, 0, 0, 0))],
        out_specs=pl.BlockSpec((1, tq, LANES), lambda b, hp, t: (b, t, hp)),
        out_shape=jax.ShapeDtypeStruct((bsz, s, A_WIDTH), BF16),
        compiler_params=_params(3),
        name="na",
    )(qa, ka, va, bias)


def _attend(streams, st_ref, p_ref):
    nck, _, tk = streams[0][2].shape
    n = streams[0][0].shape[1]

    def scores(i, j, par, h):
        q, k_ref, _ = streams[i]
        off = pl.multiple_of(j * tk + h * MXU_TILE, MXU_TILE)
        st = jnp.dot(k_ref[pl.ds(off, MXU_TILE), :], q, preferred_element_type=F32)
        st_ref[par][i, h * MXU_TILE:(h + 1) * MXU_TILE, :] = st
        c = st[0:SUBLANES]
        for r in range(SUBLANES, MXU_TILE, SUBLANES):
            c = jnp.maximum(c, st[r:r + SUBLANES])
        return c

    def softmax(i, par, m, cmax):
        m_new = jnp.maximum(m, jnp.max(cmax, axis=0, keepdims=True))
        for r in range(0, tk, SOFTMAX_SLAB):
            rows = slice(r, r + SOFTMAX_SLAB)
            p_ref[par][i, rows, :] = jnp.exp2(st_ref[par][i, rows, :] - m_new).astype(BF16)
        return m_new, jnp.exp2(m - m_new)

    def step(j, par, carry, with_scores=True, with_softmax=True):
        out = []
        for i, (m, alpha, acc, cmax) in enumerate(carry):
            acc = acc * alpha
            cmax_next = cmax
            for h in range(tk // MXU_TILE):
                if with_scores:
                    c = scores(i, j + 2, (par + 2) % RING, h)
                    cmax_next = c if h == 0 else jnp.maximum(cmax_next, c)
                rows = slice(h * MXU_TILE, (h + 1) * MXU_TILE)
                acc = acc + jnp.dot(streams[i][2][j, :, rows], p_ref[par][i, rows, :],
                                    preferred_element_type=F32)
            if with_softmax:
                m, alpha = softmax(i, (par + 1) % RING, m, cmax)
            out.append((m, alpha, acc, cmax_next))
        return tuple(out)

    def body(jj, carry):
        for u in range(STEPS_PER_TRIP):
            carry = step(STEPS_PER_TRIP * jj + u, u % RING, carry)
        return carry

    def chunk_scores(i, j, par):
        return functools.reduce(jnp.maximum, [scores(i, j, par, h) for h in range(tk // MXU_TILE)])

    carry = []
    for i in range(len(streams)):
        cmax0 = chunk_scores(i, 0, 0)
        cmax1 = chunk_scores(i, 1, 1) if nck > 1 else cmax0
        m, alpha = softmax(i, 0, jnp.full((1, n), -jnp.inf, F32), cmax0)
        carry.append((m, alpha, jnp.zeros((V_ROWS, n), F32), cmax1))
    carry = tuple(carry)
    trips = max(nck - 2, 0) // STEPS_PER_TRIP
    carry = lax.fori_loop(0, trips, body, carry)
    for j in range(STEPS_PER_TRIP * trips, nck):
        carry = step(j, j % RING, carry, with_scores=j + 2 < nck, with_softmax=j + 1 < nck)
    return [acc[0:HEAD_DIM] / acc[HEAD_DIM:HEAD_DIM + 1] for _, _, acc, _ in carry]


def _attend_scratch(n_streams, tk, n):
    return ([pltpu.VMEM((n_streams, tk, n), F32)] * RING + [pltpu.VMEM((n_streams, tk, n), BF16)] * RING)


def _tiles_per_step(nck, n_tiles):
    t = min(max(1, STEPS_PER_BLOCK // nck), MAX_TILES_PER_STEP, n_tiles)
    while n_tiles % t:
        t -= 1
    return t


def _attend_tiles(tile_streams, scratch):
    sets = len(scratch) // (2 * RING)
    outs = []
    for t, streams in enumerate(tile_streams):
        base = (t % sets) * 2 * RING
        outs.append(_attend(streams, scratch[base:base + RING], scratch[base + RING:base + 2 * RING]))
    return outs


def _gqa_kernel(q_ref, k_ref, vt_ref, o_ref, *scratch):
    tq = scratch[0].shape[-1]
    tiles = q_ref.shape[3] // tq
    outs = _attend_tiles([[(q_ref[0, h, :, t * tq:(t + 1) * tq], k_ref.at[0, 0], vt_ref.at[0, 0])
                           for h in range(q_ref.shape[1])] for t in range(tiles)], scratch)
    for t, o in enumerate(outs):
        o_ref[0, t * tq:(t + 1) * tq, :] = jnp.concatenate(o, axis=0).T.astype(BF16)


def _gqa(qb, kb, vt):
    bsz, _, _, s = qb.shape
    tq = GQA_Q_TILE
    group = B_HEADS // B_KV_HEADS
    gw = group * HEAD_DIM
    tiles = _tiles_per_step(vt.shape[2], s // tq)
    return pl.pallas_call(
        _gqa_kernel,
        grid=(bsz, B_KV_HEADS, s // (tiles * tq)),
        in_specs=[pl.BlockSpec((1, group, LANES, tiles * tq), lambda b, kv, i: (b, kv, 0, i)),
                  pl.BlockSpec((1, 1, s, LANES), lambda b, kv, i: (b, kv, 0, 0)),
                  pl.BlockSpec((1, 1) + vt.shape[2:], lambda b, kv, i: (b, kv, 0, 0, 0))],
        out_specs=pl.BlockSpec((1, tiles * tq, gw), lambda b, kv, i: (b, i, kv)),
        out_shape=jax.ShapeDtypeStruct((bsz, s, B_WIDTH), BF16),
        scratch_shapes=_attend_scratch(group, vt.shape[-1], tq) * min(tiles, 2),
        compiler_params=_params(3),
        name="gqa",
    )(qb, kb, vt)


def _mla_kernel(q_ref, k_ref, vt_ref, o_ref, *scratch):
    sub = scratch[0].shape[-1]
    per_head = scratch[0].shape[0] // 2
    tq = per_head * sub
    tiles = q_ref.shape[3] // tq
    outs = _attend_tiles([[(q_ref[0, hh, :, t * tq + c * sub:t * tq + (c + 1) * sub],
                            k_ref.at[0, hh], vt_ref.at[0, hh])
                           for hh in range(2) for c in range(per_head)] for t in range(tiles)], scratch)
    for t, o in enumerate(outs):
        ot = jnp.concatenate([jnp.concatenate(o[hh * per_head:(hh + 1) * per_head], axis=1)
                              for hh in range(2)], axis=0)
        o_ref[0, t * tq:(t + 1) * tq, :] = ot.T.astype(BF16)


def _mla(q, k, vt):
    bsz, nh, _, s = q.shape
    tq = MLA_Q_TILE
    tiles = _tiles_per_step(vt.shape[2], s // tq)
    return pl.pallas_call(
        _mla_kernel,
        grid=(bsz, nh // 2, s // (tiles * tq)),
        in_specs=[pl.BlockSpec((1, 2, LANES, tiles * tq), lambda b, hp, i: (b, hp, 0, i)),
                  pl.BlockSpec((1, 2, s, LANES), lambda b, hp, i: (b, hp, 0, 0)),
                  pl.BlockSpec((1, 2) + vt.shape[2:], lambda b, hp, i: (b, hp, 0, 0, 0))],
        out_specs=pl.BlockSpec((1, tiles * tq, LANES), lambda b, hp, i: (b, i, hp)),
        out_shape=jax.ShapeDtypeStruct((bsz, s, nh * C_V), BF16),
        scratch_shapes=_attend_scratch(2 * tq // STREAM_QUERIES, vt.shape[-1], STREAM_QUERIES) * min(tiles, 2),
        compiler_params=_params(3),
        name="mla",
    )(q, k, vt)


def _out1_kernel(x_ref, m_ref, gate_ref, w_ref, g_ref, y_ref):
    gm = (m_ref[0].astype(F32) * _silu(gate_ref[0].astype(F32))).astype(BF16)
    y = x_ref[0] + jnp.dot(gm, w_ref[...], preferred_element_type=F32)
    y_ref[0] = _rms(y, g_ref[...])


def _out1(x, m, gate, w, g):
    bsz, s, _ = x.shape
    tm = TOKEN_TILE
    tok = pl.BlockSpec((1, tm, D_MODEL), lambda b, i: (b, i, 0))
    return pl.pallas_call(
        _out1_kernel,
        grid=(bsz, s // tm),
        in_specs=[tok, tok, tok, pl.BlockSpec(w.shape, lambda b, i: (0, 0)),
                  pl.BlockSpec((1, D_MODEL), lambda b, i: (0, 0))],
        out_specs=tok,
        out_shape=jax.ShapeDtypeStruct(x.shape, F32),
        compiler_params=_params(2),
        name="out1",
    )(x, m, gate, w, g)


def _mid_kernel(x_ref, ma_ref, mb_ref, gate0_ref, wout_ref, g_ref, w_ref, qg_ref, kvg_ref,
                wuqt_ref, wuk_ref, wuvt_ref, cos_ref, sin_ref, cost_ref, sint_ref,
                y_ref, q_ref, k_ref, vt_ref, gate_ref):
    tm = x_ref.shape[1]
    sg = _silu(gate0_ref[0].astype(F32))
    ga = (ma_ref[0].astype(F32) * sg[:, 0:A_WIDTH]).astype(BF16)
    gb = (mb_ref[0].astype(F32) * sg[:, A_WIDTH:]).astype(BF16)
    y = (x_ref[0] + jnp.dot(ga, wout_ref[0:A_WIDTH, :], preferred_element_type=F32)
         + jnp.dot(gb, wout_ref[A_WIDTH:, :], preferred_element_type=F32))
    y_ref[0] = y

    h = _rms(y, g_ref[...]).astype(BF16)
    lat_w = C_Q_RANK + C_KV_RANK + LANES
    lat = jnp.dot(h, w_ref[:, 0:lat_w], preferred_element_type=F32)
    gate_ref[0] = jnp.dot(h, w_ref[:, lat_w:], preferred_element_type=F32).astype(BF16)

    cqt = _rms(lat[:, 0:C_Q_RANK], qg_ref[...]).T.astype(BF16)
    qt_all = jnp.dot(wuqt_ref[...], cqt, preferred_element_type=F32)
    cost = cost_ref[...]
    sint = sint_ref[...]
    half = C_ROPE // 2
    for hd in range(C_HEADS):
        xq = qt_all[hd * LANES:(hd + 1) * LANES]
        partner = jnp.concatenate([xq[:C_NOPE], xq[C_NOPE + half:C_QK_DIM],
                                   xq[C_NOPE:C_NOPE + half], xq[C_QK_DIM:]], axis=0)
        q_ref[0, hd] = ((xq * cost + partner * sint) * (C_QK_DIM ** -0.5 * LOG2E)).astype(BF16)

    ckv = _rms(lat[:, C_Q_RANK:C_Q_RANK + C_KV_RANK], kvg_ref[...])
    kr = lat[:, C_Q_RANK + C_KV_RANK:lat_w]
    low = lax.broadcasted_iota(jnp.int32, (tm, LANES), 1) < C_NOPE + half
    partner = jnp.where(low, pltpu.roll(kr, LANES - half, 1), pltpu.roll(kr, half, 1))
    k_rope = kr * cos_ref[...] + partner * sin_ref[...]
    k_all = jnp.dot(ckv.astype(BF16), wuk_ref[...], preferred_element_type=F32)
    for hd in range(C_HEADS):
        k_ref[0, hd] = (k_all[:, hd * LANES:(hd + 1) * LANES] + k_rope).astype(BF16)

    vt_all = jnp.dot(wuvt_ref[...], ckv.T.astype(BF16), preferred_element_type=F32)
    ones = jnp.ones((V_ROWS - C_V, tm), BF16)
    for hd in range(C_HEADS):
        vt_ref[0, hd, 0, 0:C_V, :] = vt_all[hd * C_V:(hd + 1) * C_V].astype(BF16)
        vt_ref[0, hd, 0, C_V:V_ROWS, :] = ones


def _mid(x, ma, mb, gate0, wout, g, w, qg, kvg, wuqt, wuk, wuvt, cos, sin, cost, sint):
    bsz, s, _ = x.shape
    tm = TOKEN_TILE
    nck = s // tm
    tok = lambda width: pl.BlockSpec((1, tm, width), lambda b, i: (b, i, 0))
    const = lambda a: pl.BlockSpec(a.shape, lambda b, i: (0,) * a.ndim)
    heads = lambda shape, imap: pl.BlockSpec((1, C_HEADS) + shape, imap)
    return pl.pallas_call(
        _mid_kernel,
        grid=(bsz, nck),
        in_specs=[tok(D_MODEL), tok(A_WIDTH), tok(B_WIDTH), tok(D_MODEL), const(wout),
                  const(g), const(w), const(qg), const(kvg), const(wuqt), const(wuk), const(wuvt),
                  pl.BlockSpec((tm, LANES), lambda b, i: (i, 0)),
                  pl.BlockSpec((tm, LANES), lambda b, i: (i, 0)),
                  pl.BlockSpec((LANES, tm), lambda b, i: (0, i)),
                  pl.BlockSpec((LANES, tm), lambda b, i: (0, i))],
        out_specs=[tok(D_MODEL),
                   heads((LANES, tm), lambda b, i: (b, 0, 0, i)),
                   heads((tm, LANES), lambda b, i: (b, 0, i, 0)),
                   heads((1, V_ROWS, tm), lambda b, i: (b, 0, i, 0, 0)),
                   tok(D_MODEL)],
        out_shape=[jax.ShapeDtypeStruct(x.shape, F32),
                   jax.ShapeDtypeStruct((bsz, C_HEADS, LANES, s), BF16),
                   jax.ShapeDtypeStruct((bsz, C_HEADS, s, LANES), BF16),
                   jax.ShapeDtypeStruct((bsz, C_HEADS, nck, V_ROWS, tm), BF16),
                   jax.ShapeDtypeStruct((bsz, s, D_MODEL), BF16)],
        compiler_params=_params(2),
        name="mid",
    )(x, ma, mb, gate0, wout, g, w, qg, kvg, wuqt, wuk, wuvt, cos, sin, cost, sint)


def _axial_angles(n_tok, rot_dim):
    n_freq = rot_dim // 4
    inv = ROPE_THETA ** (-jnp.arange(n_freq, dtype=F32) / n_freq)
    t = jnp.arange(n_tok, dtype=jnp.int32)
    row = (t // GRID_W).astype(F32)
    col = (t % GRID_W).astype(F32)
    ang = jnp.concatenate([row[:, None] * inv[None], col[:, None] * inv[None]], axis=-1)
    return jnp.cos(ang), jnp.sin(ang)


def _rope_tables(s):
    c, sn = _axial_angles(s, HEAD_DIM)
    cos0 = jnp.tile(jnp.concatenate([c, c], axis=-1), (1, LANES // HEAD_DIM))
    sin0 = jnp.tile(jnp.concatenate([-sn, sn], axis=-1), (1, LANES // HEAD_DIM))
    c, sn = _axial_angles(s, C_ROPE)
    pad = LANES - C_QK_DIM
    cos1 = jnp.concatenate([jnp.ones((s, C_NOPE), F32), c, c, jnp.ones((s, pad), F32)], axis=-1)
    sin1 = jnp.concatenate([jnp.zeros((s, C_NOPE), F32), -sn, sn, jnp.zeros((s, pad), F32)], axis=-1)
    return cos0, sin0, cos1, sin1


def _prepare(norm_e, w_in_e, rpb_a, qnorm_b, knorm_b, w_out_e,
             norm_o, w_in_o, qlat_g, kvlat_g, w_uq, w_ukv, w_out_o, norm_f):
    rep = LANES // HEAD_DIM
    lat = C_Q_RANK + C_KV_RANK
    z = lambda n: jnp.zeros((D_MODEL, n), F32)
    w1 = jnp.concatenate([w_in_o[0][:, :lat], z(C_NOPE), w_in_o[0][:, lat:lat + C_ROPE],
                          z(LANES - C_QK_DIM), w_in_o[0][:, lat + C_ROPE:]], axis=1)
    wuq = jnp.pad(w_uq[0].reshape(C_Q_RANK, C_HEADS, C_QK_DIM),
                  ((0, 0), (0, 0), (0, LANES - C_QK_DIM))).reshape(C_Q_RANK, C_HEADS * LANES)
    wkv = w_ukv[0].reshape(C_KV_RANK, C_HEADS, C_NOPE + C_V)
    wuk = jnp.pad(wkv[:, :, :C_NOPE], ((0, 0), (0, 0), (0, LANES - C_NOPE))).reshape(C_KV_RANK, C_HEADS * LANES)
    wuv = wkv[:, :, C_NOPE:].reshape(C_KV_RANK, C_WIDTH)
    return dict(
        norm_e=norm_e[0][None], w_in_e=w_in_e[0].astype(BF16), bias=_na_bias_table(rpb_a[0]),
        qg=jnp.tile(qnorm_b[0], rep)[None], kg=jnp.tile(knorm_b[0], rep)[None],
        w_out_e=w_out_e[0].astype(BF16),
        norm_o=norm_o[0][None], w1=w1.astype(BF16), qlat_g=qlat_g[0][None], kvlat_g=kvlat_g[0][None],
        wuqt=wuq.T.astype(BF16), wuk=wuk.astype(BF16), wuvt=wuv.T.astype(BF16),
        w_out_o=w_out_o[0].astype(BF16), norm_f=norm_f[None])


def _trunk(x, p):
    cos0, sin0, cos1, sin1 = _rope_tables(x.shape[1])
    qa, ka, va, qb, kb, vbt, gate0 = _proj0(x, p["norm_e"], p["w_in_e"], cos0, sin0, p["qg"], p["kg"])
    mix_a = _na(qa, ka, va, p["bias"])
    mix_b = _gqa(qb, kb, vbt)
    x1, q, k, vt, gate1 = _mid(x, mix_a, mix_b, gate0, p["w_out_e"], p["norm_o"], p["w1"],
                               p["qlat_g"], p["kvlat_g"], p["wuqt"], p["wuk"], p["wuvt"],
                               cos1, sin1, cos1.T, sin1.T)
    mix_c = _mla(q, k, vt)
    return _out1(x1, mix_c, gate1, p["w_out_o"], p["norm_f"])


def kernel(x_prompt, x_sample, norm_e, w_in_e, rpb_a, qnorm_b, knorm_b, w_out_e,
           norm_o, w_in_o, qlat_g, kvlat_g, w_uq, w_ukv, w_out_o, norm_f):
    assert norm_e.shape[0] == 1 and norm_o.shape[0] == 1
    p = _prepare(norm_e, w_in_e, rpb_a, qnorm_b, knorm_b, w_out_e,
                 norm_o, w_in_o, qlat_g, kvlat_g, w_uq, w_ukv, w_out_o, norm_f)
    return (_trunk(x_prompt, p), _trunk(x_sample, p))
```

```python
import functools

import numpy as np
import jax
import jax.numpy as jnp
from jax import lax
from jax.experimental import pallas as pl
from jax.experimental.pallas import tpu as pltpu

D_MODEL = 1024
GRID_W = 64
HEAD_DIM = 64
A_HEADS = 8
NA_KH = 8
NA_KW = 16
B_HEADS = 8
B_KV_HEADS = 2
C_HEADS = 16
C_NOPE = 64
C_ROPE = 32
C_V = 64
C_Q_RANK = 384
C_KV_RANK = 256
ROPE_THETA = 10000.0
EPS = 1e-6

A_WIDTH = A_HEADS * HEAD_DIM
B_WIDTH = B_HEADS * HEAD_DIM
B_KV_WIDTH = B_KV_HEADS * HEAD_DIM
C_WIDTH = C_HEADS * C_V
C_QK_DIM = C_NOPE + C_ROPE

LANES = 128
SUBLANES = 8
MXU_TILE = 256
SOFTMAX_SLAB = 32
TOKEN_TILE = 512
PROJ_ROWS = 256
V_ROWS = HEAD_DIM + 16
NA_ROWS_PER_STEP = 32
STREAM_QUERIES = 256
GQA_Q_TILE = STREAM_QUERIES
MLA_Q_TILE = 2 * STREAM_QUERIES
RING = 3
STEPS_PER_BLOCK = 32
MAX_TILES_PER_STEP = 4
STEPS_PER_TRIP = 30
assert STEPS_PER_TRIP % RING == 0
LOG2E = 1.4426950408889634
V7X_VMEM_BYTES = 64 * 1024 * 1024
VMEM_LIMIT = V7X_VMEM_BYTES * 7 // 8
MASKED = -1e30

F32 = jnp.float32
BF16 = jnp.bfloat16
_NT = (((1,), (1,)), ((), ()))


def _params(n_axes):
    return pltpu.CompilerParams(dimension_semantics=("arbitrary",) * n_axes,
                                vmem_limit_bytes=VMEM_LIMIT)


def _rms(x, g):
    return x * lax.rsqrt(jnp.mean(x * x, axis=-1, keepdims=True) + EPS) * g


def _silu(g):
    return g * jax.nn.sigmoid(g)


def _proj0_kernel(x_ref, g_ref, w_ref, cos_ref, sin_ref, qg_ref, kg_ref,
                  qa_ref, ka_ref, va_ref, qb_ref, kb_ref, vt_ref, gate_ref):
    tm = x_ref.shape[1]
    for r0 in range(0, tm, PROJ_ROWS):
        _proj0_rows(slice(r0, r0 + PROJ_ROWS), x_ref, g_ref, w_ref, cos_ref, sin_ref, qg_ref, kg_ref,
                    qa_ref, ka_ref, va_ref, qb_ref, kb_ref, vt_ref, gate_ref)


def _proj0_rows(rows, x_ref, g_ref, w_ref, cos_ref, sin_ref, qg_ref, kg_ref,
                qa_ref, ka_ref, va_ref, qb_ref, kb_ref, vt_ref, gate_ref):
    tm = rows.stop - rows.start
    h = _rms(x_ref[0, rows, :], g_ref[...]).astype(BF16)

    a = jnp.dot(h, w_ref[:, 0:3 * A_WIDTH], preferred_element_type=F32)
    qa_ref[0, rows, :] = (a[:, 0:A_WIDTH] * (HEAD_DIM ** -0.5 * LOG2E)).astype(BF16)
    ka_ref[0, rows, :] = a[:, A_WIDTH:2 * A_WIDTH].astype(BF16)
    va_ref[0, rows, :] = a[:, 2 * A_WIDTH:3 * A_WIDTH].astype(BF16)

    o = 3 * A_WIDTH
    b = jnp.dot(h, w_ref[:, o:o + B_WIDTH + 2 * B_KV_WIDTH], preferred_element_type=F32)
    cos = cos_ref[rows, :]
    sin = sin_ref[rows, :]
    lane = lax.broadcasted_iota(jnp.int32, (tm, LANES), 1)
    first = lane < HEAD_DIM
    low = (lane % HEAD_DIM) < HEAD_DIM // 2

    def norm_rope(xc, g):
        sq = xc * xc
        sa = jnp.sum(jnp.where(first, sq, 0.0), axis=-1, keepdims=True)
        sb = jnp.sum(jnp.where(first, 0.0, sq), axis=-1, keepdims=True)
        ms = jnp.where(first, sa, sb) * (1.0 / HEAD_DIM)
        y = xc * lax.rsqrt(ms + EPS) * g
        partner = jnp.where(low, pltpu.roll(y, LANES - HEAD_DIM // 2, 1),
                            pltpu.roll(y, HEAD_DIM // 2, 1))
        return y * cos + partner * sin

    qg = qg_ref[...]
    for j in range(B_WIDTH // LANES):
        qj = norm_rope(b[:, j * LANES:(j + 1) * LANES], qg)
        qt = (qj * (HEAD_DIM ** -0.5 * LOG2E)).T.astype(BF16)
        upper = lax.broadcasted_iota(jnp.int32, qt.shape, 0) < HEAD_DIM
        qb_ref[0, 2 * j, :, rows] = jnp.where(upper, qt, jnp.zeros_like(qt))
        qb_ref[0, 2 * j + 1, :, rows] = jnp.where(upper, jnp.zeros_like(qt), qt)
    kk = norm_rope(b[:, B_WIDTH:B_WIDTH + LANES], kg_ref[...])
    ks = pltpu.roll(kk, HEAD_DIM, 1)
    kb_ref[0, 0, rows, :] = jnp.where(first, kk, ks).astype(BF16)
    kb_ref[0, 1, rows, :] = jnp.where(first, ks, kk).astype(BF16)
    vt = b[:, B_WIDTH + LANES:B_WIDTH + 2 * LANES].T
    ones = jnp.ones((V_ROWS - HEAD_DIM, tm), BF16)
    for kv in range(B_KV_HEADS):
        vt_ref[0, kv, 0, 0:HEAD_DIM, rows] = vt[kv * HEAD_DIM:(kv + 1) * HEAD_DIM].astype(BF16)
        vt_ref[0, kv, 0, HEAD_DIM:V_ROWS, rows] = ones

    o += B_WIDTH + 2 * B_KV_WIDTH
    gate_ref[0, rows, :] = jnp.dot(h, w_ref[:, o:o + D_MODEL], preferred_element_type=F32).astype(BF16)


def _proj0(x, g, w, cos, sin, qg, kg):
    bsz, s, _ = x.shape
    tm = TOKEN_TILE
    nck = s // tm
    tok = lambda width: pl.BlockSpec((1, tm, width), lambda b, i: (b, i, 0))
    const = lambda shape: pl.BlockSpec(shape, lambda b, i: (0,) * len(shape))
    return pl.pallas_call(
        _proj0_kernel,
        grid=(bsz, nck),
        in_specs=[tok(D_MODEL), const((1, D_MODEL)), const(w.shape),
                  pl.BlockSpec((tm, LANES), lambda b, i: (i, 0)),
                  pl.BlockSpec((tm, LANES), lambda b, i: (i, 0)),
                  const((1, LANES)), const((1, LANES))],
        out_specs=[tok(A_WIDTH), tok(A_WIDTH), tok(A_WIDTH),
                   pl.BlockSpec((1, B_HEADS, LANES, tm), lambda b, i: (b, 0, 0, i)),
                   pl.BlockSpec((1, B_KV_HEADS, tm, LANES), lambda b, i: (b, 0, i, 0)),
                   pl.BlockSpec((1, B_KV_HEADS, 1, V_ROWS, tm), lambda b, i: (b, 0, i, 0, 0)),
                   tok(D_MODEL)],
        out_shape=[jax.ShapeDtypeStruct((bsz, s, A_WIDTH), BF16)] * 3
        + [jax.ShapeDtypeStruct((bsz, B_HEADS, LANES, s), BF16),
           jax.ShapeDtypeStruct((bsz, B_KV_HEADS, s, LANES), BF16),
           jax.ShapeDtypeStruct((bsz, B_KV_HEADS, nck, V_ROWS, tm), BF16),
           jax.ShapeDtypeStruct((bsz, s, D_MODEL), BF16)],
        compiler_params=_params(2),
        name="proj0",
    )(x, g, w, cos, sin, qg, kg)


def _na_kernel(q_ref, k_ref, v_ref, bias_ref, o_ref, *, rows):
    t = pl.program_id(2)
    rows_per_step = q_ref.shape[1] // GRID_W
    lane = lax.broadcasted_iota(jnp.int32, (GRID_W, LANES), 1)
    first = lane < HEAD_DIM
    win = NA_KH * GRID_W

    def scores(i):
        r = t * rows_per_step + i
        rs = jnp.clip(r - NA_KH // 2, 0, rows - NA_KH)
        case = r - rs
        koff = pl.multiple_of(rs * GRID_W, GRID_W)
        q = q_ref[0, i * GRID_W:(i + 1) * GRID_W, :]
        zero = jnp.zeros_like(q)
        q2 = jnp.concatenate([jnp.where(first, q, zero), jnp.where(first, zero, q)], axis=0)
        bias = jnp.concatenate([bias_ref[0, case], bias_ref[1, case]], axis=0)
        sc = lax.dot_general(q2, k_ref[0, pl.ds(koff, win), :], _NT, preferred_element_type=F32) + bias
        return koff, sc

    def softmax(koff, sc):
        e = jnp.exp2(sc - jnp.max(sc, axis=-1, keepdims=True))
        return koff, e.astype(BF16), jnp.sum(e, axis=-1, keepdims=True)

    def attend(i, koff, e, l):
        o2 = jnp.dot(e, v_ref[0, pl.ds(koff, win), :], preferred_element_type=F32) / l
        o_ref[0, i * GRID_W:(i + 1) * GRID_W, :] = jnp.where(first, o2[:GRID_W], o2[GRID_W:]).astype(BF16)

    scored = probs = None
    for i in range(rows_per_step + 2):
        nxt = scores(i) if i < rows_per_step else None
        if probs is not None:
            attend(i - 2, *probs)
        probs = softmax(*scored) if scored is not None else None
        scored = nxt


def _na_bias_table(rpb):
    qc = np.arange(GRID_W)[:, None]
    kc = np.arange(GRID_W)[None, :]
    cs = np.clip(qc - NA_KW // 2, 0, GRID_W - NA_KW)
    valid = (kc >= cs) & (kc < cs + NA_KW)
    onehot = ((kc - qc + NA_KW - 1)[..., None] == np.arange(2 * NA_KW - 1)) & valid[..., None]
    cols = jnp.einsum("hrd,qkd->hrqk", rpb * LOG2E, jnp.asarray(onehot, F32),
                      precision=lax.Precision.HIGHEST)
    cols = jnp.where(valid[None, None], cols, MASKED)
    tbl = jnp.stack([cols[:, NA_KH - 1 - c:2 * NA_KH - 1 - c] for c in range(NA_KH)], axis=1)
    return tbl.transpose(0, 1, 3, 2, 4).reshape(rpb.shape[0], NA_KH, GRID_W, NA_KH * GRID_W)


def _na(qa, ka, va, bias):
    bsz, s, _ = qa.shape
    rows = s // GRID_W
    rows_per_step = min(NA_ROWS_PER_STEP, rows)
    assert rows >= NA_KH and rows % rows_per_step == 0
    tq = rows_per_step * GRID_W
    return pl.pallas_call(
        functools.partial(_na_kernel, rows=rows),
        grid=(bsz, A_WIDTH // LANES, s // tq),
        in_specs=[pl.BlockSpec((1, tq, LANES), lambda b, hp, t: (b, t, hp)),
                  pl.BlockSpec((1, s, LANES), lambda b, hp, t: (b, 0, hp)),
                  pl.BlockSpec((1, s, LANES), lambda b, hp, t: (b, 0, hp)),
                  pl.BlockSpec((2, NA_KH, GRID_W, NA_KH * GRID_W), lambda b, hp, t: (hp, 0, 0, 0))],
        out_specs=pl.BlockSpec((1, tq, LANES), lambda b, hp, t: (b, t, hp)),
        out_shape=jax.ShapeDtypeStruct((bsz, s, A_WIDTH), BF16),
        compiler_params=_params(3),
        name="na",
    )(qa, ka, va, bias)


def _attend(streams, st_ref, p_ref):
    nck, _, tk = streams[0][2].shape
    n = streams[0][0].shape[1]

    def scores(i, j, par, h):
        q, k_ref, _ = streams[i]
        off = pl.multiple_of(j * tk + h * MXU_TILE, MXU_TILE)
        st = jnp.dot(k_ref[pl.ds(off, MXU_TILE), :], q, preferred_element_type=F32)
        st_ref[par][i, h * MXU_TILE:(h + 1) * MXU_TILE, :] = st
        c = st[0:SUBLANES]
        for r in range(SUBLANES, MXU_TILE, SUBLANES):
            c = jnp.maximum(c, st[r:r + SUBLANES])
        return c

    def softmax(i, par, m, cmax):
        m_new = jnp.maximum(m, jnp.max(cmax, axis=0, keepdims=True))
        for r in range(0, tk, SOFTMAX_SLAB):
            rows = slice(r, r + SOFTMAX_SLAB)
            p_ref[par][i, rows, :] = jnp.exp2(st_ref[par][i, rows, :] - m_new).astype(BF16)
        return m_new, jnp.exp2(m - m_new)

    def step(j, par, carry, with_scores=True, with_softmax=True):
        out = []
        for i, (m, alpha, acc, cmax) in enumerate(carry):
            acc = acc * alpha
            cmax_next = cmax
            for h in range(tk // MXU_TILE):
                if with_scores:
                    c = scores(i, j + 2, (par + 2) % RING, h)
                    cmax_next = c if h == 0 else jnp.maximum(cmax_next, c)
                rows = slice(h * MXU_TILE, (h + 1) * MXU_TILE)
                acc = acc + jnp.dot(streams[i][2][j, :, rows], p_ref[par][i, rows, :],
                                    preferred_element_type=F32)
            if with_softmax:
                m, alpha = softmax(i, (par + 1) % RING, m, cmax)
            out.append((m, alpha, acc, cmax_next))
        return tuple(out)

    def body(jj, carry):
        for u in range(STEPS_PER_TRIP):
            carry = step(STEPS_PER_TRIP * jj + u, u % RING, carry)
        return carry

    def chunk_scores(i, j, par):
        return functools.reduce(jnp.maximum, [scores(i, j, par, h) for h in range(tk // MXU_TILE)])

    carry = []
    for i in range(len(streams)):
        cmax0 = chunk_scores(i, 0, 0)
        cmax1 = chunk_scores(i, 1, 1) if nck > 1 else cmax0
        m, alpha = softmax(i, 0, jnp.full((1, n), -jnp.inf, F32), cmax0)
        carry.append((m, alpha, jnp.zeros((V_ROWS, n), F32), cmax1))
    carry = tuple(carry)
    trips = max(nck - 2, 0) // STEPS_PER_TRIP
    carry = lax.fori_loop(0, trips, body, carry)
    for j in range(STEPS_PER_TRIP * trips, nck):
        carry = step(j, j % RING, carry, with_scores=j + 2 < nck, with_softmax=j + 1 < nck)
    return [acc[0:HEAD_DIM] / acc[HEAD_DIM:HEAD_DIM + 1] for _, _, acc, _ in carry]


def _attend_scratch(n_streams, tk, n):
    return ([pltpu.VMEM((n_streams, tk, n), F32)] * RING + [pltpu.VMEM((n_streams, tk, n), BF16)] * RING)


def _tiles_per_step(nck, n_tiles):
    t = min(max(1, STEPS_PER_BLOCK // nck), MAX_TILES_PER_STEP, n_tiles)
    while n_tiles % t:
        t -= 1
    return t


def _attend_tiles(tile_streams, scratch):
    sets = len(scratch) // (2 * RING)
    outs = []
    for t, streams in enumerate(tile_streams):
        base = (t % sets) * 2 * RING
        outs.append(_attend(streams, scratch[base:base + RING], scratch[base + RING:base + 2 * RING]))
    return outs


def _gqa_kernel(q_ref, k_ref, vt_ref, o_ref, *scratch):
    tq = scratch[0].shape[-1]
    tiles = q_ref.shape[3] // tq
    outs = _attend_tiles([[(q_ref[0, h, :, t * tq:(t + 1) * tq], k_ref.at[0, 0], vt_ref.at[0, 0])
                           for h in range(q_ref.shape[1])] for t in range(tiles)], scratch)
    for t, o in enumerate(outs):
        o_ref[0, t * tq:(t + 1) * tq, :] = jnp.concatenate(o, axis=0).T.astype(BF16)


def _gqa(qb, kb, vt):
    bsz, _, _, s = qb.shape
    tq = GQA_Q_TILE
    group = B_HEADS // B_KV_HEADS
    gw = group * HEAD_DIM
    tiles = _tiles_per_step(vt.shape[2], s // tq)
    return pl.pallas_call(
        _gqa_kernel,
        grid=(bsz, B_KV_HEADS, s // (tiles * tq)),
        in_specs=[pl.BlockSpec((1, group, LANES, tiles * tq), lambda b, kv, i: (b, kv, 0, i)),
                  pl.BlockSpec((1, 1, s, LANES), lambda b, kv, i: (b, kv, 0, 0)),
                  pl.BlockSpec((1, 1) + vt.shape[2:], lambda b, kv, i: (b, kv, 0, 0, 0))],
        out_specs=pl.BlockSpec((1, tiles * tq, gw), lambda b, kv, i: (b, i, kv)),
        out_shape=jax.ShapeDtypeStruct((bsz, s, B_WIDTH), BF16),
        scratch_shapes=_attend_scratch(group, vt.shape[-1], tq) * min(tiles, 2),
        compiler_params=_params(3),
        name="gqa",
    )(qb, kb, vt)


def _mla_kernel(q_ref, k_ref, vt_ref, o_ref, *scratch):
    sub = scratch[0].shape[-1]
    per_head = scratch[0].shape[0] // 2
    tq = per_head * sub
    tiles = q_ref.shape[3] // tq
    outs = _attend_tiles([[(q_ref[0, hh, :, t * tq + c * sub:t * tq + (c + 1) * sub],
                            k_ref.at[0, hh], vt_ref.at[0, hh])
                           for hh in range(2) for c in range(per_head)] for t in range(tiles)], scratch)
    for t, o in enumerate(outs):
        ot = jnp.concatenate([jnp.concatenate(o[hh * per_head:(hh + 1) * per_head], axis=1)
                              for hh in range(2)], axis=0)
        o_ref[0, t * tq:(t + 1) * tq, :] = ot.T.astype(BF16)


def _mla(q, k, vt):
    bsz, nh, _, s = q.shape
    tq = MLA_Q_TILE
    tiles = _tiles_per_step(vt.shape[2], s // tq)
    return pl.pallas_call(
        _mla_kernel,
        grid=(bsz, nh // 2, s // (tiles * tq)),
        in_specs=[pl.BlockSpec((1, 2, LANES, tiles * tq), lambda b, hp, i: (b, hp, 0, i)),
                  pl.BlockSpec((1, 2, s, LANES), lambda b, hp, i: (b, hp, 0, 0)),
                  pl.BlockSpec((1, 2) + vt.shape[2:], lambda b, hp, i: (b, hp, 0, 0, 0))],
        out_specs=pl.BlockSpec((1, tiles * tq, LANES), lambda b, hp, i: (b, i, hp)),
        out_shape=jax.ShapeDtypeStruct((bsz, s, nh * C_V), BF16),
        scratch_shapes=_attend_scratch(2 * tq // STREAM_QUERIES, vt.shape[-1], STREAM_QUERIES) * min(tiles, 2),
        compiler_params=_params(3),
        name="mla",
    )(q, k, vt)


def _out1_kernel(x_ref, m_ref, gate_ref, w_ref, g_ref, y_ref):
    gm = (m_ref[0].astype(F32) * _silu(gate_ref[0].astype(F32))).astype(BF16)
    y = x_ref[0] + jnp.dot(gm, w_ref[...], preferred_element_type=F32)
    y_ref[0] = _rms(y, g_ref[...])


def _out1(x, m, gate, w, g):
    bsz, s, _ = x.shape
    tm = TOKEN_TILE
    tok = pl.BlockSpec((1, tm, D_MODEL), lambda b, i: (b, i, 0))
    return pl.pallas_call(
        _out1_kernel,
        grid=(bsz, s // tm),
        in_specs=[tok, tok, tok, pl.BlockSpec(w.shape, lambda b, i: (0, 0)),
                  pl.BlockSpec((1, D_MODEL), lambda b, i: (0, 0))],
        out_specs=tok,
        out_shape=jax.ShapeDtypeStruct(x.shape, F32),
        compiler_params=_params(2),
        name="out1",
    )(x, m, gate, w, g)


def _mid_kernel(x_ref, ma_ref, mb_ref, gate0_ref, wout_ref, g_ref, w_ref, qg_ref, kvg_ref,
                wuqt_ref, wuk_ref, wuvt_ref, cos_ref, sin_ref, cost_ref, sint_ref,
                y_ref, q_ref, k_ref, vt_ref, gate_ref):
    tm = x_ref.shape[1]
    sg = _silu(gate0_ref[0].astype(F32))
    ga = (ma_ref[0].astype(F32) * sg[:, 0:A_WIDTH]).astype(BF16)
    gb = (mb_ref[0].astype(F32) * sg[:, A_WIDTH:]).astype(BF16)
    y = (x_ref[0] + jnp.dot(ga, wout_ref[0:A_WIDTH, :], preferred_element_type=F32)
         + jnp.dot(gb, wout_ref[A_WIDTH:, :], preferred_element_type=F32))
    y_ref[0] = y

    h = _rms(y, g_ref[...]).astype(BF16)
    lat_w = C_Q_RANK + C_KV_RANK + LANES
    lat = jnp.dot(h, w_ref[:, 0:lat_w], preferred_element_type=F32)
    gate_ref[0] = jnp.dot(h, w_ref[:, lat_w:], preferred_element_type=F32).astype(BF16)

    cqt = _rms(lat[:, 0:C_Q_RANK], qg_ref[...]).T.astype(BF16)
    qt_all = jnp.dot(wuqt_ref[...], cqt, preferred_element_type=F32)
    cost = cost_ref[...]
    sint = sint_ref[...]
    half = C_ROPE // 2
    for hd in range(C_HEADS):
        xq = qt_all[hd * LANES:(hd + 1) * LANES]
        partner = jnp.concatenate([xq[:C_NOPE], xq[C_NOPE + half:C_QK_DIM],
                                   xq[C_NOPE:C_NOPE + half], xq[C_QK_DIM:]], axis=0)
        q_ref[0, hd] = ((xq * cost + partner * sint) * (C_QK_DIM ** -0.5 * LOG2E)).astype(BF16)

    ckv = _rms(lat[:, C_Q_RANK:C_Q_RANK + C_KV_RANK], kvg_ref[...])
    kr = lat[:, C_Q_RANK + C_KV_RANK:lat_w]
    low = lax.broadcasted_iota(jnp.int32, (tm, LANES), 1) < C_NOPE + half
    partner = jnp.where(low, pltpu.roll(kr, LANES - half, 1), pltpu.roll(kr, half, 1))
    k_rope = kr * cos_ref[...] + partner * sin_ref[...]
    k_all = jnp.dot(ckv.astype(BF16), wuk_ref[...], preferred_element_type=F32)
    for hd in range(C_HEADS):
        k_ref[0, hd] = (k_all[:, hd * LANES:(hd + 1) * LANES] + k_rope).astype(BF16)

    vt_all = jnp.dot(wuvt_ref[...], ckv.T.astype(BF16), preferred_element_type=F32)
    ones = jnp.ones((V_ROWS - C_V, tm), BF16)
    for hd in range(C_HEADS):
        vt_ref[0, hd, 0, 0:C_V, :] = vt_all[hd * C_V:(hd + 1) * C_V].astype(BF16)
        vt_ref[0, hd, 0, C_V:V_ROWS, :] = ones


def _mid(x, ma, mb, gate0, wout, g, w, qg, kvg, wuqt, wuk, wuvt, cos, sin, cost, sint):
    bsz, s, _ = x.shape
    tm = TOKEN_TILE
    nck = s // tm
    tok = lambda width: pl.BlockSpec((1, tm, width), lambda b, i: (b, i, 0))
    const = lambda a: pl.BlockSpec(a.shape, lambda b, i: (0,) * a.ndim)
    heads = lambda shape, imap: pl.BlockSpec((1, C_HEADS) + shape, imap)
    return pl.pallas_call(
        _mid_kernel,
        grid=(bsz, nck),
        in_specs=[tok(D_MODEL), tok(A_WIDTH), tok(B_WIDTH), tok(D_MODEL), const(wout),
                  const(g), const(w), const(qg), const(kvg), const(wuqt), const(wuk), const(wuvt),
                  pl.BlockSpec((tm, LANES), lambda b, i: (i, 0)),
                  pl.BlockSpec((tm, LANES), lambda b, i: (i, 0)),
                  pl.BlockSpec((LANES, tm), lambda b, i: (0, i)),
                  pl.BlockSpec((LANES, tm), lambda b, i: (0, i))],
        out_specs=[tok(D_MODEL),
                   heads((LANES, tm), lambda b, i: (b, 0, 0, i)),
                   heads((tm, LANES), lambda b, i: (b, 0, i, 0)),
                   heads((1, V_ROWS, tm), lambda b, i: (b, 0, i, 0, 0)),
                   tok(D_MODEL)],
        out_shape=[jax.ShapeDtypeStruct(x.shape, F32),
                   jax.ShapeDtypeStruct((bsz, C_HEADS, LANES, s), BF16),
                   jax.ShapeDtypeStruct((bsz, C_HEADS, s, LANES), BF16),
                   jax.ShapeDtypeStruct((bsz, C_HEADS, nck, V_ROWS, tm), BF16),
                   jax.ShapeDtypeStruct((bsz, s, D_MODEL), BF16)],
        compiler_params=_params(2),
        name="mid",
    )(x, ma, mb, gate0, wout, g, w, qg, kvg, wuqt, wuk, wuvt, cos, sin, cost, sint)


def _axial_angles(n_tok, rot_dim):
    n_freq = rot_dim // 4
    inv = ROPE_THETA ** (-jnp.arange(n_freq, dtype=F32) / n_freq)
    t = jnp.arange(n_tok, dtype=jnp.int32)
    row = (t // GRID_W).astype(F32)
    col = (t % GRID_W).astype(F32)
    ang = jnp.concatenate([row[:, None] * inv[None], col[:, None] * inv[None]], axis=-1)
    return jnp.cos(ang), jnp.sin(ang)


def _rope_tables(s):
    c, sn = _axial_angles(s, HEAD_DIM)
    cos0 = jnp.tile(jnp.concatenate([c, c], axis=-1), (1, LANES // HEAD_DIM))
    sin0 = jnp.tile(jnp.concatenate([-sn, sn], axis=-1), (1, LANES // HEAD_DIM))
    c, sn = _axial_angles(s, C_ROPE)
    pad = LANES - C_QK_DIM
    cos1 = jnp.concatenate([jnp.ones((s, C_NOPE), F32), c, c, jnp.ones((s, pad), F32)], axis=-1)
    sin1 = jnp.concatenate([jnp.zeros((s, C_NOPE), F32), -sn, sn, jnp.zeros((s, pad), F32)], axis=-1)
    return cos0, sin0, cos1, sin1


def _prepare(norm_e, w_in_e, rpb_a, qnorm_b, knorm_b, w_out_e,
             norm_o, w_in_o, qlat_g, kvlat_g, w_uq, w_ukv, w_out_o, norm_f):
    rep = LANES // HEAD_DIM
    lat = C_Q_RANK + C_KV_RANK
    z = lambda n: jnp.zeros((D_MODEL, n), F32)
    w1 = jnp.concatenate([w_in_o[0][:, :lat], z(C_NOPE), w_in_o[0][:, lat:lat + C_ROPE],
                          z(LANES - C_QK_DIM), w_in_o[0][:, lat + C_ROPE:]], axis=1)
    wuq = jnp.pad(w_uq[0].reshape(C_Q_RANK, C_HEADS, C_QK_DIM),
                  ((0, 0), (0, 0), (0, LANES - C_QK_DIM))).reshape(C_Q_RANK, C_HEADS * LANES)
    wkv = w_ukv[0].reshape(C_KV_RANK, C_HEADS, C_NOPE + C_V)
    wuk = jnp.pad(wkv[:, :, :C_NOPE], ((0, 0), (0, 0), (0, LANES - C_NOPE))).reshape(C_KV_RANK, C_HEADS * LANES)
    wuv = wkv[:, :, C_NOPE:].reshape(C_KV_RANK, C_WIDTH)
    return dict(
        norm_e=norm_e[0][None], w_in_e=w_in_e[0].astype(BF16), bias=_na_bias_table(rpb_a[0]),
        qg=jnp.tile(qnorm_b[0], rep)[None], kg=jnp.tile(knorm_b[0], rep)[None],
        w_out_e=w_out_e[0].astype(BF16),
        norm_o=norm_o[0][None], w1=w1.astype(BF16), qlat_g=qlat_g[0][None], kvlat_g=kvlat_g[0][None],
        wuqt=wuq.T.astype(BF16), wuk=wuk.astype(BF16), wuvt=wuv.T.astype(BF16),
        w_out_o=w_out_o[0].astype(BF16), norm_f=norm_f[None])


def _trunk(x, p):
    cos0, sin0, cos1, sin1 = _rope_tables(x.shape[1])
    qa, ka, va, qb, kb, vbt, gate0 = _proj0(x, p["norm_e"], p["w_in_e"], cos0, sin0, p["qg"], p["kg"])
    mix_a = _na(qa, ka, va, p["bias"])
    mix_b = _gqa(qb, kb, vbt)
    x1, q, k, vt, gate1 = _mid(x, mix_a, mix_b, gate0, p["w_out_e"], p["norm_o"], p["w1"],
                               p["qlat_g"], p["kvlat_g"], p["wuqt"], p["wuk"], p["wuvt"],
                               cos1, sin1, cos1.T, sin1.T)
    mix_c = _mla(q, k, vt)
    return _out1(x1, mix_c, gate1, p["w_out_o"], p["norm_f"])


def kernel(x_prompt, x_sample, norm_e, w_in_e, rpb_a, qnorm_b, knorm_b, w_out_e,
           norm_o, w_in_o, qlat_g, kvlat_g, w_uq, w_ukv, w_out_o, norm_f):
    assert norm_e.shape[0] == 1 and norm_o.shape[0] == 1
    p = _prepare(norm_e, w_in_e, rpb_a, qnorm_b, knorm_b, w_out_e,
                 norm_o, w_in_o, qlat_g, kvlat_g, w_uq, w_ukv, w_out_o, norm_f)
    return (_trunk(x_prompt, p), _trunk(x_sample, p))
```

```python
import functools

import numpy as np
import jax
import jax.numpy as jnp
from jax import lax
from jax.experimental import pallas as pl
from jax.experimental.pallas import tpu as pltpu

D_MODEL = 1024
GRID_W = 64
HEAD_DIM = 64
A_HEADS = 8
NA_KH = 8
NA_KW = 16
B_HEADS = 8
B_KV_HEADS = 2
C_HEADS = 16
C_NOPE = 64
C_ROPE = 32
C_V = 64
C_Q_RANK = 384
C_KV_RANK = 256
ROPE_THETA = 10000.0
EPS = 1e-6

A_WIDTH = A_HEADS * HEAD_DIM
B_WIDTH = B_HEADS * HEAD_DIM
B_KV_WIDTH = B_KV_HEADS * HEAD_DIM
C_WIDTH = C_HEADS * C_V
C_QK_DIM = C_NOPE + C_ROPE

LANES = 128
SUBLANES = 8
MXU_TILE = 256
SOFTMAX_SLAB = 32
TOKEN_TILE = 512
PROJ_ROWS = 256
V_ROWS = HEAD_DIM + 16
NA_ROWS_PER_STEP = 32
STREAM_QUERIES = 256
GQA_Q_TILE = STREAM_QUERIES
MLA_Q_TILE = 2 * STREAM_QUERIES
RING = 3
STEPS_PER_BLOCK = 32
MAX_TILES_PER_STEP = 4
STEPS_PER_TRIP = 30
assert STEPS_PER_TRIP % RING == 0
LOG2E = 1.4426950408889634
V7X_VMEM_BYTES = 64 * 1024 * 1024
VMEM_LIMIT = V7X_VMEM_BYTES * 7 // 8
MASKED = -1e30

F32 = jnp.float32
BF16 = jnp.bfloat16
_NT = (((1,), (1,)), ((), ()))


def _params(n_axes):
    return pltpu.CompilerParams(dimension_semantics=("arbitrary",) * n_axes,
                                vmem_limit_bytes=VMEM_LIMIT)


def _rms(x, g):
    return x * lax.rsqrt(jnp.mean(x * x, axis=-1, keepdims=True) + EPS) * g


def _silu(g):
    return g * jax.nn.sigmoid(g)


def _proj0_kernel(x_ref, g_ref, w_ref, cos_ref, sin_ref, qg_ref, kg_ref,
                  qa_ref, ka_ref, va_ref, qb_ref, kb_ref, vt_ref, gate_ref):
    tm = x_ref.shape[1]
    for r0 in range(0, tm, PROJ_ROWS):
        _proj0_rows(slice(r0, r0 + PROJ_ROWS), x_ref, g_ref, w_ref, cos_ref, sin_ref, qg_ref, kg_ref,
                    qa_ref, ka_ref, va_ref, qb_ref, kb_ref, vt_ref, gate_ref)


def _proj0_rows(rows, x_ref, g_ref, w_ref, cos_ref, sin_ref, qg_ref, kg_ref,
                qa_ref, ka_ref, va_ref, qb_ref, kb_ref, vt_ref, gate_ref):
    tm = rows.stop - rows.start
    h = _rms(x_ref[0, rows, :], g_ref[...]).astype(BF16)

    a = jnp.dot(h, w_ref[:, 0:3 * A_WIDTH], preferred_element_type=F32)
    qa_ref[0, rows, :] = (a[:, 0:A_WIDTH] * (HEAD_DIM ** -0.5 * LOG2E)).astype(BF16)
    ka_ref[0, rows, :] = a[:, A_WIDTH:2 * A_WIDTH].astype(BF16)
    va_ref[0, rows, :] = a[:, 2 * A_WIDTH:3 * A_WIDTH].astype(BF16)

    o = 3 * A_WIDTH
    b = jnp.dot(h, w_ref[:, o:o + B_WIDTH + 2 * B_KV_WIDTH], preferred_element_type=F32)
    cos = cos_ref[rows, :]
    sin = sin_ref[rows, :]
    lane = lax.broadcasted_iota(jnp.int32, (tm, LANES), 1)
    first = lane < HEAD_DIM
    low = (lane % HEAD_DIM) < HEAD_DIM // 2

    def norm_rope(xc, g):
        sq = xc * xc
        sa = jnp.sum(jnp.where(first, sq, 0.0), axis=-1, keepdims=True)
        sb = jnp.sum(jnp.where(first, 0.0, sq), axis=-1, keepdims=True)
        ms = jnp.where(first, sa, sb) * (1.0 / HEAD_DIM)
        y = xc * lax.rsqrt(ms + EPS) * g
        partner = jnp.where(low, pltpu.roll(y, LANES - HEAD_DIM // 2, 1),
                            pltpu.roll(y, HEAD_DIM // 2, 1))
        return y * cos + partner * sin

    qg = qg_ref[...]
    for j in range(B_WIDTH // LANES):
        qj = norm_rope(b[:, j * LANES:(j + 1) * LANES], qg)
        qt = (qj * (HEAD_DIM ** -0.5 * LOG2E)).T.astype(BF16)
        upper = lax.broadcasted_iota(jnp.int32, qt.shape, 0) < HEAD_DIM
        qb_ref[0, 2 * j, :, rows] = jnp.where(upper, qt, jnp.zeros_like(qt))
        qb_ref[0, 2 * j + 1, :, rows] = jnp.where(upper, jnp.zeros_like(qt), qt)
    kk = norm_rope(b[:, B_WIDTH:B_WIDTH + LANES], kg_ref[...])
    ks = pltpu.roll(kk, HEAD_DIM, 1)
    kb_ref[0, 0, rows, :] = jnp.where(first, kk, ks).astype(BF16)
    kb_ref[0, 1, rows, :] = jnp.where(first, ks, kk).astype(BF16)
    vt = b[:, B_WIDTH + LANES:B_WIDTH + 2 * LANES].T
    ones = jnp.ones((V_ROWS - HEAD_DIM, tm), BF16)
    for kv in range(B_KV_HEADS):
        vt_ref[0, kv, 0, 0:HEAD_DIM, rows] = vt[kv * HEAD_DIM:(kv + 1) * HEAD_DIM].astype(BF16)
        vt_ref[0, kv, 0, HEAD_DIM:V_ROWS, rows] = ones

    o += B_WIDTH + 2 * B_KV_WIDTH
    gate_ref[0, rows, :] = jnp.dot(h, w_ref[:, o:o + D_MODEL], preferred_element_type=F32).astype(BF16)


def _proj0(x, g, w, cos, sin, qg, kg):
    bsz, s, _ = x.shape
    tm = TOKEN_TILE
    nck = s // tm
    tok = lambda width: pl.BlockSpec((1, tm, width), lambda b, i: (b, i, 0))
    const = lambda shape: pl.BlockSpec(shape, lambda b, i: (0,) * len(shape), pipeline_mode=pl.Buffered(1))
    return pl.pallas_call(
        _proj0_kernel,
        grid=(bsz, nck),
        in_specs=[tok(D_MODEL), const((1, D_MODEL)), const(w.shape),
                  pl.BlockSpec((tm, LANES), lambda b, i: (i, 0)),
                  pl.BlockSpec((tm, LANES), lambda b, i: (i, 0)),
                  const((1, LANES)), const((1, LANES))],
        out_specs=[tok(A_WIDTH), tok(A_WIDTH), tok(A_WIDTH),
                   pl.BlockSpec((1, B_HEADS, LANES, tm), lambda b, i: (b, 0, 0, i)),
                   pl.BlockSpec((1, B_KV_HEADS, tm, LANES), lambda b, i: (b, 0, i, 0)),
                   pl.BlockSpec((1, B_KV_HEADS, 1, V_ROWS, tm), lambda b, i: (b, 0, i, 0, 0)),
                   tok(D_MODEL)],
        out_shape=[jax.ShapeDtypeStruct((bsz, s, A_WIDTH), BF16)] * 3
        + [jax.ShapeDtypeStruct((bsz, B_HEADS, LANES, s), BF16),
           jax.ShapeDtypeStruct((bsz, B_KV_HEADS, s, LANES), BF16),
           jax.ShapeDtypeStruct((bsz, B_KV_HEADS, nck, V_ROWS, tm), BF16),
           jax.ShapeDtypeStruct((bsz, s, D_MODEL), BF16)],
        compiler_params=_params(2),
        name="proj0",
    )(x, g, w, cos, sin, qg, kg)


def _na_kernel(q_ref, k_ref, v_ref, bias_ref, o_ref, *, rows):
    t = pl.program_id(2)
    rows_per_step = q_ref.shape[1] // GRID_W
    lane = lax.broadcasted_iota(jnp.int32, (GRID_W, LANES), 1)
    first = lane < HEAD_DIM
    win = NA_KH * GRID_W

    def scores(i):
        r = t * rows_per_step + i
        rs = jnp.clip(r - NA_KH // 2, 0, rows - NA_KH)
        case = r - rs
        koff = pl.multiple_of(rs * GRID_W, GRID_W)
        q = q_ref[0, i * GRID_W:(i + 1) * GRID_W, :]
        zero = jnp.zeros_like(q)
        q2 = jnp.concatenate([jnp.where(first, q, zero), jnp.where(first, zero, q)], axis=0)
        bias = jnp.concatenate([bias_ref[0, case], bias_ref[1, case]], axis=0)
        sc = lax.dot_general(q2, k_ref[0, pl.ds(koff, win), :], _NT, preferred_element_type=F32) + bias
        return koff, sc

    def softmax(koff, sc):
        e = jnp.exp2(sc - jnp.max(sc, axis=-1, keepdims=True))
        return koff, e.astype(BF16), jnp.sum(e, axis=-1, keepdims=True)

    def attend(i, koff, e, l):
        o2 = jnp.dot(e, v_ref[0, pl.ds(koff, win), :], preferred_element_type=F32) / l
        o_ref[0, i * GRID_W:(i + 1) * GRID_W, :] = jnp.where(first, o2[:GRID_W], o2[GRID_W:]).astype(BF16)

    scored = probs = None
    for i in range(rows_per_step + 2):
        nxt = scores(i) if i < rows_per_step else None
        if probs is not None:
            attend(i - 2, *probs)
        probs = softmax(*scored) if scored is not None else None
        scored = nxt


def _na_bias_table(rpb):
    qc = np.arange(GRID_W)[:, None]
    kc = np.arange(GRID_W)[None, :]
    cs = np.clip(qc - NA_KW // 2, 0, GRID_W - NA_KW)
    valid = (kc >= cs) & (kc < cs + NA_KW)
    onehot = ((kc - qc + NA_KW - 1)[..., None] == np.arange(2 * NA_KW - 1)) & valid[..., None]
    cols = jnp.einsum("hrd,qkd->hrqk", rpb * LOG2E, jnp.asarray(onehot, F32),
                      precision=lax.Precision.HIGHEST)
    cols = jnp.where(valid[None, None], cols, MASKED)
    tbl = jnp.stack([cols[:, NA_KH - 1 - c:2 * NA_KH - 1 - c] for c in range(NA_KH)], axis=1)
    return tbl.transpose(0, 1, 3, 2, 4).reshape(rpb.shape[0], NA_KH, GRID_W, NA_KH * GRID_W)


def _na(qa, ka, va, bias):
    bsz, s, _ = qa.shape
    rows = s // GRID_W
    rows_per_step = min(NA_ROWS_PER_STEP, rows)
    assert rows >= NA_KH and rows % rows_per_step == 0
    tq = rows_per_step * GRID_W
    return pl.pallas_call(
        functools.partial(_na_kernel, rows=rows),
        grid=(bsz, A_WIDTH // LANES, s // tq),
        in_specs=[pl.BlockSpec((1, tq, LANES), lambda b, hp, t: (b, t, hp)),
                  pl.BlockSpec((1, s, LANES), lambda b, hp, t: (b, 0, hp)),
                  pl.BlockSpec((1, s, LANES), lambda b, hp, t: (b, 0, hp)),
                  pl.BlockSpec((2, NA_KH, GRID_W, NA_KH * GRID_W), lambda b, hp, t: (hp, 0, 0, 0))],
        out_specs=pl.BlockSpec((1, tq, LANES), lambda b, hp, t: (b, t, hp)),
        out_shape=jax.ShapeDtypeStruct((bsz, s, A_WIDTH), BF16),
        compiler_params=_params(3),
        name="na",
    )(qa, ka, va, bias)


def _attend(streams, st_ref, p_ref):
    nck, _, tk = streams[0][2].shape
    n = streams[0][0].shape[1]

    def scores(i, j, par, h):
        q, k_ref, _ = streams[i]
        off = pl.multiple_of(j * tk + h * MXU_TILE, MXU_TILE)
        st = jnp.dot(k_ref[pl.ds(off, MXU_TILE), :], q, preferred_element_type=F32)
        st_ref[par][i, h * MXU_TILE:(h + 1) * MXU_TILE, :] = st
        c = st[0:SUBLANES]
        for r in range(SUBLANES, MXU_TILE, SUBLANES):
            c = jnp.maximum(c, st[r:r + SUBLANES])
        return c

    def softmax(i, par, m, cmax):
        m_new = jnp.maximum(m, jnp.max(cmax, axis=0, keepdims=True))
        for r in range(0, tk, SOFTMAX_SLAB):
            rows = slice(r, r + SOFTMAX_SLAB)
            p_ref[par][i, rows, :] = jnp.exp2(st_ref[par][i, rows, :] - m_new).astype(BF16)
        return m_new, jnp.exp2(m - m_new)

    def step(j, par, carry, with_scores=True, with_softmax=True):
        out = []
        for i, (m, alpha, acc, cmax) in enumerate(carry):
            acc = acc * alpha
            cmax_next = cmax
            for h in range(tk // MXU_TILE):
                if with_scores:
                    c = scores(i, j + 2, (par + 2) % RING, h)
                    cmax_next = c if h == 0 else jnp.maximum(cmax_next, c)
                rows = slice(h * MXU_TILE, (h + 1) * MXU_TILE)
                acc = acc + jnp.dot(streams[i][2][j, :, rows], p_ref[par][i, rows, :],
                                    preferred_element_type=F32)
            if with_softmax:
                m, alpha = softmax(i, (par + 1) % RING, m, cmax)
            out.append((m, alpha, acc, cmax_next))
        return tuple(out)

    def body(jj, carry):
        for u in range(STEPS_PER_TRIP):
            carry = step(STEPS_PER_TRIP * jj + u, u % RING, carry)
        return carry

    def chunk_scores(i, j, par):
        return functools.reduce(jnp.maximum, [scores(i, j, par, h) for h in range(tk // MXU_TILE)])

    carry = []
    for i in range(len(streams)):
        cmax0 = chunk_scores(i, 0, 0)
        cmax1 = chunk_scores(i, 1, 1) if nck > 1 else cmax0
        m, alpha = softmax(i, 0, jnp.full((1, n), -jnp.inf, F32), cmax0)
        carry.append((m, alpha, jnp.zeros((V_ROWS, n), F32), cmax1))
    carry = tuple(carry)
    trips = max(nck - 2, 0) // STEPS_PER_TRIP
    carry = lax.fori_loop(0, trips, body, carry)
    for j in range(STEPS_PER_TRIP * trips, nck):
        carry = step(j, j % RING, carry, with_scores=j + 2 < nck, with_softmax=j + 1 < nck)
    return [acc[0:HEAD_DIM] / acc[HEAD_DIM:HEAD_DIM + 1] for _, _, acc, _ in carry]


def _attend_scratch(n_streams, tk, n):
    return ([pltpu.VMEM((n_streams, tk, n), F32)] * RING + [pltpu.VMEM((n_streams, tk, n), BF16)] * RING)


def _tiles_per_step(nck, n_tiles):
    t = min(max(1, STEPS_PER_BLOCK // nck), MAX_TILES_PER_STEP, n_tiles)
    while n_tiles % t:
        t -= 1
    return t


def _attend_tiles(tile_streams, scratch):
    sets = len(scratch) // (2 * RING)
    outs = []
    for t, streams in enumerate(tile_streams):
        base = (t % sets) * 2 * RING
        outs.append(_attend(streams, scratch[base:base + RING], scratch[base + RING:base + 2 * RING]))
    return outs


def _gqa_kernel(q_ref, k_ref, vt_ref, o_ref, *scratch):
    tq = scratch[0].shape[-1]
    tiles = q_ref.shape[3] // tq
    outs = _attend_tiles([[(q_ref[0, h, :, t * tq:(t + 1) * tq], k_ref.at[0, 0], vt_ref.at[0, 0])
                           for h in range(q_ref.shape[1])] for t in range(tiles)], scratch)
    for t, o in enumerate(outs):
        o_ref[0, t * tq:(t + 1) * tq, :] = jnp.concatenate(o, axis=0).T.astype(BF16)


def _gqa(qb, kb, vt):
    bsz, _, _, s = qb.shape
    tq = GQA_Q_TILE
    group = B_HEADS // B_KV_HEADS
    gw = group * HEAD_DIM
    tiles = _tiles_per_step(vt.shape[2], s // tq)
    return pl.pallas_call(
        _gqa_kernel,
        grid=(bsz, B_KV_HEADS, s // (tiles * tq)),
        in_specs=[pl.BlockSpec((1, group, LANES, tiles * tq), lambda b, kv, i: (b, kv, 0, i)),
                  pl.BlockSpec((1, 1, s, LANES), lambda b, kv, i: (b, kv, 0, 0)),
                  pl.BlockSpec((1, 1) + vt.shape[2:], lambda b, kv, i: (b, kv, 0, 0, 0))],
        out_specs=pl.BlockSpec((1, tiles * tq, gw), lambda b, kv, i: (b, i, kv)),
        out_shape=jax.ShapeDtypeStruct((bsz, s, B_WIDTH), BF16),
        scratch_shapes=_attend_scratch(group, vt.shape[-1], tq) * min(tiles, 2),
        compiler_params=_params(3),
        name="gqa",
    )(qb, kb, vt)


def _mla_kernel(q_ref, k_ref, vt_ref, o_ref, *scratch):
    sub = scratch[0].shape[-1]
    per_head = scratch[0].shape[0] // 2
    tq = per_head * sub
    tiles = q_ref.shape[3] // tq
    outs = _attend_tiles([[(q_ref[0, hh, :, t * tq + c * sub:t * tq + (c + 1) * sub],
                            k_ref.at[0, hh], vt_ref.at[0, hh])
                           for hh in range(2) for c in range(per_head)] for t in range(tiles)], scratch)
    for t, o in enumerate(outs):
        ot = jnp.concatenate([jnp.concatenate(o[hh * per_head:(hh + 1) * per_head], axis=1)
                              for hh in range(2)], axis=0)
        o_ref[0, t * tq:(t + 1) * tq, :] = ot.T.astype(BF16)


def _mla(q, k, vt):
    bsz, nh, _, s = q.shape
    tq = MLA_Q_TILE
    tiles = _tiles_per_step(vt.shape[2], s // tq)
    return pl.pallas_call(
        _mla_kernel,
        grid=(bsz, nh // 2, s // (tiles * tq)),
        in_specs=[pl.BlockSpec((1, 2, LANES, tiles * tq), lambda b, hp, i: (b, hp, 0, i)),
                  pl.BlockSpec((1, 2, s, LANES), lambda b, hp, i: (b, hp, 0, 0)),
                  pl.BlockSpec((1, 2) + vt.shape[2:], lambda b, hp, i: (b, hp, 0, 0, 0))],
        out_specs=pl.BlockSpec((1, tiles * tq, LANES), lambda b, hp, i: (b, i, hp)),
        out_shape=jax.ShapeDtypeStruct((bsz, s, nh * C_V), BF16),
        scratch_shapes=_attend_scratch(2 * tq // STREAM_QUERIES, vt.shape[-1], STREAM_QUERIES) * min(tiles, 2),
        compiler_params=_params(3),
        name="mla",
    )(q, k, vt)


def _out1_kernel(x_ref, m_ref, gate_ref, w_ref, g_ref, y_ref):
    gm = (m_ref[0].astype(F32) * _silu(gate_ref[0].astype(F32))).astype(BF16)
    y = x_ref[0] + jnp.dot(gm, w_ref[...], preferred_element_type=F32)
    y_ref[0] = _rms(y, g_ref[...])


def _out1(x, m, gate, w, g):
    bsz, s, _ = x.shape
    tm = TOKEN_TILE
    tok = pl.BlockSpec((1, tm, D_MODEL), lambda b, i: (b, i, 0))
    return pl.pallas_call(
        _out1_kernel,
        grid=(bsz, s // tm),
        in_specs=[tok, tok, tok, pl.BlockSpec(w.shape, lambda b, i: (0, 0), pipeline_mode=pl.Buffered(1)),
                  pl.BlockSpec((1, D_MODEL), lambda b, i: (0, 0), pipeline_mode=pl.Buffered(1))],
        out_specs=tok,
        out_shape=jax.ShapeDtypeStruct(x.shape, F32),
        compiler_params=_params(2),
        name="out1",
    )(x, m, gate, w, g)


def _mid_kernel(x_ref, ma_ref, mb_ref, gate0_ref, wout_ref, g_ref, w_ref, qg_ref, kvg_ref,
                wuqt_ref, wuk_ref, wuvt_ref, cos_ref, sin_ref, cost_ref, sint_ref,
                y_ref, q_ref, k_ref, vt_ref, gate_ref):
    tm = x_ref.shape[1]
    sg = _silu(gate0_ref[0].astype(F32))
    ga = (ma_ref[0].astype(F32) * sg[:, 0:A_WIDTH]).astype(BF16)
    gb = (mb_ref[0].astype(F32) * sg[:, A_WIDTH:]).astype(BF16)
    y = (x_ref[0] + jnp.dot(ga, wout_ref[0:A_WIDTH, :], preferred_element_type=F32)
         + jnp.dot(gb, wout_ref[A_WIDTH:, :], preferred_element_type=F32))
    y_ref[0] = y

    h = _rms(y, g_ref[...]).astype(BF16)
    lat_w = C_Q_RANK + C_KV_RANK + LANES
    lat = jnp.dot(h, w_ref[:, 0:lat_w], preferred_element_type=F32)
    gate_ref[0] = jnp.dot(h, w_ref[:, lat_w:], preferred_element_type=F32).astype(BF16)

    cqt = _rms(lat[:, 0:C_Q_RANK], qg_ref[...]).T.astype(BF16)
    qt_all = jnp.dot(wuqt_ref[...], cqt, preferred_element_type=F32)
    cost = cost_ref[...]
    sint = sint_ref[...]
    half = C_ROPE // 2
    for hd in range(C_HEADS):
        xq = qt_all[hd * LANES:(hd + 1) * LANES]
        partner = jnp.concatenate([xq[:C_NOPE], xq[C_NOPE + half:C_QK_DIM],
                                   xq[C_NOPE:C_NOPE + half], xq[C_QK_DIM:]], axis=0)
        q_ref[0, hd] = ((xq * cost + partner * sint) * (C_QK_DIM ** -0.5 * LOG2E)).astype(BF16)

    ckv = _rms(lat[:, C_Q_RANK:C_Q_RANK + C_KV_RANK], kvg_ref[...])
    kr = lat[:, C_Q_RANK + C_KV_RANK:lat_w]
    low = lax.broadcasted_iota(jnp.int32, (tm, LANES), 1) < C_NOPE + half
    partner = jnp.where(low, pltpu.roll(kr, LANES - half, 1), pltpu.roll(kr, half, 1))
    k_rope = kr * cos_ref[...] + partner * sin_ref[...]
    k_all = jnp.dot(ckv.astype(BF16), wuk_ref[...], preferred_element_type=F32)
    for hd in range(C_HEADS):
        k_ref[0, hd] = (k_all[:, hd * LANES:(hd + 1) * LANES] + k_rope).astype(BF16)

    vt_all = jnp.dot(wuvt_ref[...], ckv.T.astype(BF16), preferred_element_type=F32)
    ones = jnp.ones((V_ROWS - C_V, tm), BF16)
    for hd in range(C_HEADS):
        vt_ref[0, hd, 0, 0:C_V, :] = vt_all[hd * C_V:(hd + 1) * C_V].astype(BF16)
        vt_ref[0, hd, 0, C_V:V_ROWS, :] = ones


def _mid(x, ma, mb, gate0, wout, g, w, qg, kvg, wuqt, wuk, wuvt, cos, sin, cost, sint):
    bsz, s, _ = x.shape
    tm = TOKEN_TILE
    nck = s // tm
    tok = lambda width: pl.BlockSpec((1, tm, width), lambda b, i: (b, i, 0))
    const = lambda a: pl.BlockSpec(a.shape, lambda b, i: (0,) * a.ndim, pipeline_mode=pl.Buffered(1))
    heads = lambda shape, imap: pl.BlockSpec((1, C_HEADS) + shape, imap)
    return pl.pallas_call(
        _mid_kernel,
        grid=(bsz, nck),
        in_specs=[tok(D_MODEL), tok(A_WIDTH), tok(B_WIDTH), tok(D_MODEL), const(wout),
                  const(g), const(w), const(qg), const(kvg), const(wuqt), const(wuk), const(wuvt),
                  pl.BlockSpec((tm, LANES), lambda b, i: (i, 0)),
                  pl.BlockSpec((tm, LANES), lambda b, i: (i, 0)),
                  pl.BlockSpec((LANES, tm), lambda b, i: (0, i)),
                  pl.BlockSpec((LANES, tm), lambda b, i: (0, i))],
        out_specs=[tok(D_MODEL),
                   heads((LANES, tm), lambda b, i: (b, 0, 0, i)),
                   heads((tm, LANES), lambda b, i: (b, 0, i, 0)),
                   heads((1, V_ROWS, tm), lambda b, i: (b, 0, i, 0, 0)),
                   tok(D_MODEL)],
        out_shape=[jax.ShapeDtypeStruct(x.shape, F32),
                   jax.ShapeDtypeStruct((bsz, C_HEADS, LANES, s), BF16),
                   jax.ShapeDtypeStruct((bsz, C_HEADS, s, LANES), BF16),
                   jax.ShapeDtypeStruct((bsz, C_HEADS, nck, V_ROWS, tm), BF16),
                   jax.ShapeDtypeStruct((bsz, s, D_MODEL), BF16)],
        compiler_params=_params(2),
        name="mid",
    )(x, ma, mb, gate0, wout, g, w, qg, kvg, wuqt, wuk, wuvt, cos, sin, cost, sint)


def _axial_angles(n_tok, rot_dim):
    n_freq = rot_dim // 4
    inv = ROPE_THETA ** (-jnp.arange(n_freq, dtype=F32) / n_freq)
    t = jnp.arange(n_tok, dtype=jnp.int32)
    row = (t // GRID_W).astype(F32)
    col = (t % GRID_W).astype(F32)
    ang = jnp.concatenate([row[:, None] * inv[None], col[:, None] * inv[None]], axis=-1)
    return jnp.cos(ang), jnp.sin(ang)


def _rope_tables(s):
    c, sn = _axial_angles(s, HEAD_DIM)
    cos0 = jnp.tile(jnp.concatenate([c, c], axis=-1), (1, LANES // HEAD_DIM))
    sin0 = jnp.tile(jnp.concatenate([-sn, sn], axis=-1), (1, LANES // HEAD_DIM))
    c, sn = _axial_angles(s, C_ROPE)
    pad = LANES - C_QK_DIM
    cos1 = jnp.concatenate([jnp.ones((s, C_NOPE), F32), c, c, jnp.ones((s, pad), F32)], axis=-1)
    sin1 = jnp.concatenate([jnp.zeros((s, C_NOPE), F32), -sn, sn, jnp.zeros((s, pad), F32)], axis=-1)
    return cos0, sin0, cos1, sin1


def _prepare(norm_e, w_in_e, rpb_a, qnorm_b, knorm_b, w_out_e,
             norm_o, w_in_o, qlat_g, kvlat_g, w_uq, w_ukv, w_out_o, norm_f):
    rep = LANES // HEAD_DIM
    lat = C_Q_RANK + C_KV_RANK
    z = lambda n: jnp.zeros((D_MODEL, n), F32)
    w1 = jnp.concatenate([w_in_o[0][:, :lat], z(C_NOPE), w_in_o[0][:, lat:lat + C_ROPE],
                          z(LANES - C_QK_DIM), w_in_o[0][:, lat + C_ROPE:]], axis=1)
    wuq = jnp.pad(w_uq[0].reshape(C_Q_RANK, C_HEADS, C_QK_DIM),
                  ((0, 0), (0, 0), (0, LANES - C_QK_DIM))).reshape(C_Q_RANK, C_HEADS * LANES)
    wkv = w_ukv[0].reshape(C_KV_RANK, C_HEADS, C_NOPE + C_V)
    wuk = jnp.pad(wkv[:, :, :C_NOPE], ((0, 0), (0, 0), (0, LANES - C_NOPE))).reshape(C_KV_RANK, C_HEADS * LANES)
    wuv = wkv[:, :, C_NOPE:].reshape(C_KV_RANK, C_WIDTH)
    return dict(
        norm_e=norm_e[0][None], w_in_e=w_in_e[0].astype(BF16), bias=_na_bias_table(rpb_a[0]),
        qg=jnp.tile(qnorm_b[0], rep)[None], kg=jnp.tile(knorm_b[0], rep)[None],
        w_out_e=w_out_e[0].astype(BF16),
        norm_o=norm_o[0][None], w1=w1.astype(BF16), qlat_g=qlat_g[0][None], kvlat_g=kvlat_g[0][None],
        wuqt=wuq.T.astype(BF16), wuk=wuk.astype(BF16), wuvt=wuv.T.astype(BF16),
        w_out_o=w_out_o[0].astype(BF16), norm_f=norm_f[None])


def _trunk(x, p):
    cos0, sin0, cos1, sin1 = _rope_tables(x.shape[1])
    qa, ka, va, qb, kb, vbt, gate0 = _proj0(x, p["norm_e"], p["w_in_e"], cos0, sin0, p["qg"], p["kg"])
    mix_a = _na(qa, ka, va, p["bias"])
    mix_b = _gqa(qb, kb, vbt)
    x1, q, k, vt, gate1 = _mid(x, mix_a, mix_b, gate0, p["w_out_e"], p["norm_o"], p["w1"],
                               p["qlat_g"], p["kvlat_g"], p["wuqt"], p["wuk"], p["wuvt"],
                               cos1, sin1, cos1.T, sin1.T)
    mix_c = _mla(q, k, vt)
    return _out1(x1, mix_c, gate1, p["w_out_o"], p["norm_f"])


def kernel(x_prompt, x_sample, norm_e, w_in_e, rpb_a, qnorm_b, knorm_b, w_out_e,
           norm_o, w_in_o, qlat_g, kvlat_g, w_uq, w_ukv, w_out_o, norm_f):
    assert norm_e.shape[0] == 1 and norm_o.shape[0] == 1
    p = _prepare(norm_e, w_in_e, rpb_a, qnorm_b, knorm_b, w_out_e,
                 norm_o, w_in_o, qlat_g, kvlat_g, w_uq, w_ukv, w_out_o, norm_f)
    return (_trunk(x_prompt, p), _trunk(x_sample, p))
```
